```python
import jax, jax.numpy as jnp
from jax import lax
import numpy as np

D_MODEL = 1024
BATCH = 16
SEQ = 2048
DEPTH = 1

CHUNK = 64
HEAD_DIM = 64
SWA_HEADS = 8
SWA_KV_HEADS = 2
SWA_GROUP = SWA_HEADS // SWA_KV_HEADS
WINDOW = 128
WINDOW_CHUNKS = WINDOW // CHUNK
SWA_BLOCK = 128
SB_HEADS = 8
SB_BLOCK = 128
NUM_BUCKETS = 32
MAX_DISTANCE = 128
N_EXPERTS = 256
TOP_K = 8
N_GROUPS = 8
TOPK_GROUPS = 4
EXPERT_DIM = 256
SHARED_DIM = 256
ROUTED_SCALE = 2.5
MOE_BLOCK = 128
RMS_EPS = 1e-6
N_MOD = 6
NEG_INF = -1e30

QA_W = SWA_HEADS * HEAD_DIM
KVA_W = SWA_KV_HEADS * HEAD_DIM
QB_W = SB_HEADS * HEAD_DIM
IN_WIDTH = QA_W + 2 * KVA_W + 3 * QB_W + 2 * D_MODEL
SPLITS = [QA_W, QA_W + KVA_W, QA_W + 2 * KVA_W, QA_W + 2 * KVA_W + QB_W,
          QA_W + 2 * KVA_W + 2 * QB_W, QA_W + 2 * KVA_W + 3 * QB_W,
          QA_W + 2 * KVA_W + 3 * QB_W + D_MODEL]

kernel_name = "hybrid_swa_sink_stickbreak_moe_block"


def rmsnorm(x, g):
    xf = x.astype(jnp.float32)
    y = xf * lax.rsqrt(jnp.mean(xf * xf, axis=-1, keepdims=True) + RMS_EPS)
    return (y * g.astype(jnp.float32)).astype(x.dtype)


def t5_buckets():
    i = np.arange(SWA_BLOCK)[:, None]
    j = np.arange(2 * SWA_BLOCK)[None, :]
    rel = (j - SWA_BLOCK) - i
    nb = NUM_BUCKETS // 2
    bucket = (rel > 0).astype(np.int32) * nb
    n = np.abs(rel)
    max_exact = nb // 2
    large = max_exact + (np.log(np.maximum(n, 1) / max_exact)
                         / np.log(MAX_DISTANCE / max_exact) * (nb - max_exact)).astype(np.int32)
    large = np.minimum(large, nb - 1)
    return (bucket + np.where(n < max_exact, n, large)).astype(np.int32)


def swa_sink_attention(q, k, v, sinks, rel_bias):
    B, S = q.shape[0], q.shape[1]
    nb = S // SWA_BLOCK
    qb = q.reshape(B, nb, SWA_BLOCK, SWA_KV_HEADS, SWA_GROUP, HEAD_DIM)

    def band(t):
        tb = t.reshape(B, nb, SWA_BLOCK, SWA_KV_HEADS, HEAD_DIM)
        prev = jnp.concatenate([jnp.zeros_like(tb[:, :1]), tb[:, :-1]], axis=1)
        return jnp.concatenate([prev, tb], axis=2)

    kb, vb = band(k), band(v)
    logits = jnp.einsum('bnqhgd,bnkhd->bnhgqk', qb, kb,
                        preferred_element_type=jnp.float32) * (HEAD_DIM ** -0.5)
    bias = rel_bias.astype(jnp.float32)[t5_buckets()]
    bias = bias.transpose(2, 0, 1).reshape(SWA_KV_HEADS, SWA_GROUP, SWA_BLOCK, 2 * SWA_BLOCK)
    logits = logits + bias
    blk = jnp.arange(nb)[:, None, None] * SWA_BLOCK
    qpos = blk + jnp.arange(SWA_BLOCK)[None, :, None]
    kpos = blk - SWA_BLOCK + jnp.arange(2 * SWA_BLOCK)[None, None, :]
    qc, kc = qpos // CHUNK, kpos // CHUNK
    valid = (kpos >= 0) & (kc <= qc) & (kc >= qc - WINDOW_CHUNKS)
    logits = jnp.where(valid[None, :, None, None], logits, NEG_INF)
    sink_col = jnp.broadcast_to(
        sinks.astype(jnp.float32).reshape(1, 1, SWA_KV_HEADS, SWA_GROUP, 1, 1),
        logits.shape[:-1] + (1,))
    probs = jax.nn.softmax(jnp.concatenate([logits, sink_col], axis=-1), axis=-1)[..., :-1]
    out = jnp.einsum('bnhgqk,bnkhd->bnqhgd', probs.astype(v.dtype), vb)
    return out.reshape(B, S, QA_W)


def stick_breaking_attention(q, k, v):
    B, S, H, dh = q.shape
    outs = []
    for blk in range(S // SB_BLOCK):
        q0 = blk * SB_BLOCK
        L = q0 + SB_BLOCK
        z = jnp.einsum('bqhd,bkhd->bhqk', q[:, q0:L], k[:, :L],
                       preferred_element_type=jnp.float32) * (dh ** -0.5)
        causal = jnp.arange(L)[None, :] < (q0 + jnp.arange(SB_BLOCK))[:, None]
        log_keep = jnp.where(causal, jax.nn.log_sigmoid(-z), 0.0)
        log_rest = lax.cumsum(log_keep, axis=3, reverse=True) - log_keep
        w = jnp.where(causal, jnp.exp(jax.nn.log_sigmoid(z) + log_rest), 0.0)
        outs.append(jnp.einsum('bhqk,bkhd->bqhd', w.astype(v.dtype), v[:, :L]))
    return jnp.concatenate(outs, axis=1).reshape(B, S, H * dh)


def route(h, w_router, router_bias):
    T = h.shape[0]
    scores = jax.nn.sigmoid(jnp.matmul(h, w_router, preferred_element_type=jnp.float32))
    choice = scores + router_bias.astype(jnp.float32)
    grouped = choice.reshape(T, N_GROUPS, N_EXPERTS // N_GROUPS)
    group_score = lax.top_k(grouped, 2)[0].sum(-1)
    _, top_groups = lax.top_k(group_score, TOPK_GROUPS)
    group_mask = jnp.any(top_groups[..., None] == jnp.arange(N_GROUPS)[None, None, :], axis=1)
    expert_mask = jnp.repeat(group_mask, N_EXPERTS // N_GROUPS, axis=1)
    _, idx = lax.top_k(jnp.where(expert_mask, choice, -jnp.inf), TOP_K)
    w = jnp.take_along_axis(scores, idx, axis=1)
    w = w / jnp.sum(w, axis=-1, keepdims=True) * ROUTED_SCALE
    return idx, w


def routed_experts(h, idx, wts, w_gate, w_up, w_down):
    T, D = h.shape
    N = T * TOP_K
    e_flat = idx.reshape(N)
    tok_flat = jnp.arange(N, dtype=jnp.int32) // TOP_K
    w_flat = wts.reshape(N)
    order = jnp.argsort(e_flat)
    e_sorted, tok_sorted, w_sorted = e_flat[order], tok_flat[order], w_flat[order]
    counts = jnp.bincount(e_flat, length=N_EXPERTS)
    start = jnp.cumsum(counts) - counts
    pad_counts = (counts + MOE_BLOCK - 1) // MOE_BLOCK * MOE_BLOCK
    pad_end = jnp.cumsum(pad_counts)
    pad_start = pad_end - pad_counts
    pos = pad_start[e_sorted] + jnp.arange(N) - start[e_sorted]
    n_blocks = -(-N // MOE_BLOCK) + N_EXPERTS
    tok_pad = jnp.full((n_blocks * MOE_BLOCK,), T, jnp.int32).at[pos].set(tok_sorted)
    w_pad = jnp.zeros((n_blocks * MOE_BLOCK,), h.dtype).at[pos].set(w_sorted.astype(h.dtype))
    block_e = jnp.minimum(
        jnp.searchsorted(pad_end, jnp.arange(n_blocks) * MOE_BLOCK, side='right'), N_EXPERTS - 1)
    h_pad = jnp.concatenate([h, jnp.zeros((1, D), h.dtype)], axis=0)

    def body(acc, blk):
        tok, wt, e = blk
        xb = h_pad[tok]
        hid = jax.nn.silu(xb @ w_gate[e]) * (xb @ w_up[e])
        yb = hid @ w_down[e]
        return acc.at[tok].add(yb * wt[:, None]), None

    acc, _ = lax.scan(body, jnp.zeros((T + 1, D), h.dtype),
                      (tok_pad.reshape(n_blocks, MOE_BLOCK), w_pad.reshape(n_blocks, MOE_BLOCK), block_e))
    return acc[:T]


def setup_inputs(seed: int = 0) -> dict:
    key = jax.random.key(seed)
    ks = jax.random.split(key, 24)
    f32 = jnp.float32
    D, L = D_MODEL, DEPTH

    def nrm(k, shape, scale):
        return jax.random.normal(k, shape, f32) * scale

    return {
        "x": nrm(ks[0], (BATCH, SEQ, D), 1.0),
        "c": nrm(ks[1], (BATCH, D), 1.0),
        "w_ada": nrm(ks[2], (L, D, N_MOD * D), 0.5 * D ** -0.5),
        "b_ada": nrm(ks[3], (L, N_MOD * D), 0.02),
        "g_pre_mix": 1.0 + nrm(ks[4], (L, D), 0.05),
        "g_post_mix": 1.0 + nrm(ks[5], (L, D), 0.05),
        "w_in": nrm(ks[6], (L, D, IN_WIDTH), D ** -0.5),
        "attn_sinks": nrm(ks[7], (L, SWA_HEADS), 0.5),
        "rel_bias": nrm(ks[8], (NUM_BUCKETS, SWA_HEADS), 0.5),
        "w_branch_a": nrm(ks[9], (L, QA_W, D), QA_W ** -0.5),
        "w_branch_b": nrm(ks[10], (L, QB_W, D), QB_W ** -0.5),
        "w_out": nrm(ks[11], (L, D, D), D ** -0.5),
        "g_pre_ffn": 1.0 + nrm(ks[12], (L, D), 0.05),
        "g_post_ffn": 1.0 + nrm(ks[13], (L, D), 0.05),
        "w_router": nrm(ks[14], (L, D, N_EXPERTS), D ** -0.5),
        "router_bias": nrm(ks[15], (L, N_EXPERTS), 0.01),
        "w_gate_e": nrm(ks[16], (L, N_EXPERTS, D, EXPERT_DIM), D ** -0.5),
        "w_up_e": nrm(ks[17], (L, N_EXPERTS, D, EXPERT_DIM), D ** -0.5),
        "w_down_e": nrm(ks[18], (L, N_EXPERTS, EXPERT_DIM, D), EXPERT_DIM ** -0.5),
        "w_gate_s": nrm(ks[19], (L, D, SHARED_DIM), D ** -0.5),
        "w_up_s": nrm(ks[20], (L, D, SHARED_DIM), D ** -0.5),
        "w_down_s": nrm(ks[21], (L, SHARED_DIM, D), SHARED_DIM ** -0.5),
    }


def reference(x, c, w_ada, b_ada, g_pre_mix, g_post_mix, w_in, attn_sinks, rel_bias,
              w_branch_a, w_branch_b, w_out, g_pre_ffn, g_post_ffn, w_router, router_bias,
              w_gate_e, w_up_e, w_down_e, w_gate_s, w_up_s, w_down_s):
    B, S, D = x.shape
    for l in range(DEPTH):
        mod = (jax.nn.silu(c) @ w_ada[l] + b_ada[l]).reshape(B, N_MOD, 1, D)
        shift1, scale1, gate1 = mod[:, 0], mod[:, 1], mod[:, 2]
        shift2, scale2, gate2 = mod[:, 3], mod[:, 4], mod[:, 5]

        h = rmsnorm(x, g_pre_mix[l]) * (1 + scale1) + shift1
        proj = h @ w_in[l]
        qa, ka, va, qb, kb, vb, ga, gb = jnp.split(proj, SPLITS, axis=-1)
        ya = swa_sink_attention(qa.reshape(B, S, SWA_HEADS, HEAD_DIM),
                                ka.reshape(B, S, SWA_KV_HEADS, HEAD_DIM),
                                va.reshape(B, S, SWA_KV_HEADS, HEAD_DIM),
                                attn_sinks[l], rel_bias)
        yb = stick_breaking_attention(qb.reshape(B, S, SB_HEADS, HEAD_DIM),
                                      kb.reshape(B, S, SB_HEADS, HEAD_DIM),
                                      vb.reshape(B, S, SB_HEADS, HEAD_DIM))
        merged = jax.nn.sigmoid(ga) * (ya @ w_branch_a[l]) + jax.nn.sigmoid(gb) * (yb @ w_branch_b[l])
        x = x + gate1 * rmsnorm(merged @ w_out[l], g_post_mix[l])

        h2 = (rmsnorm(x, g_pre_ffn[l]) * (1 + scale2) + shift2).reshape(B * S, D)
        idx, wts = route(h2, w_router[l], router_bias[l])
        y = routed_experts(h2, idx, wts, w_gate_e[l], w_up_e[l], w_down_e[l])
        y = y + (jax.nn.silu(h2 @ w_gate_s[l]) * (h2 @ w_up_s[l])) @ w_down_s[l]
        x = x + gate2 * rmsnorm(y.reshape(B, S, D), g_post_ffn[l])
    return x
```

```python
import functools

import numpy as np
import jax
import jax.numpy as jnp
from jax import lax
from jax.experimental import pallas as pl
from jax.experimental.pallas import tpu as pltpu

F32 = jnp.float32
BF16 = jnp.bfloat16
I32 = jnp.int32

D_MODEL = 1024
CHUNK = 64
HEAD_DIM = 64
SWA_HEADS = 8
SWA_KV_HEADS = 2
SWA_BLOCK = 128
WINDOW_CHUNKS = 2
SB_HEADS = 8
SB_BLOCK = 128
NUM_BUCKETS = 32
MAX_DISTANCE = 128
N_EXPERTS = 256
TOP_K = 8
N_GROUPS = 8
GROUP_SIZE = N_EXPERTS // N_GROUPS
TOPK_GROUPS = 4
EXPERT_DIM = 256
ROUTED_SCALE = 2.5
MOE_BLOCK = 128
RMS_EPS = 1e-6
N_MOD = 6
NEG_INF = -1e30

QA_W = SWA_HEADS * HEAD_DIM
KVA_W = SWA_KV_HEADS * HEAD_DIM
QB_W = SB_HEADS * HEAD_DIM
IN_WIDTH = QA_W + 2 * KVA_W + 3 * QB_W + 2 * D_MODEL
LANES = 128
HEAD_PAIRS = SB_HEADS // 2
Q_SCALE = HEAD_DIM ** -0.5

VMEM_LIMIT = 56 * 1024 * 1024


def _cparams(n_axes, vmem=VMEM_LIMIT):
    return pltpu.CompilerParams(dimension_semantics=("arbitrary",) * n_axes, vmem_limit_bytes=vmem)


def _dot(a, b):
    return jnp.dot(a, b, preferred_element_type=F32)


def _dot_t(a, b):
    return lax.dot_general(a, b, (((1,), (1,)), ((), ())), preferred_element_type=F32)


def _split(x):
    hi = x.astype(BF16)
    lo = (x - hi.astype(F32)).astype(BF16)
    return hi, lo


def _dot3(a, b_hi, b_lo):
    a_hi, a_lo = _split(a)
    return _dot(a_hi, b_hi) + (_dot(a_hi, b_lo) + _dot(a_lo, b_hi))


def _rmsnorm(x, g):
    return x * lax.rsqrt(jnp.mean(x * x, axis=-1, keepdims=True) + RMS_EPS) * g


def _silu(x):
    return x * jax.nn.sigmoid(x)


def _ada_kernel(c_ref, wh_ref, wl_ref, b_ref, o_ref):
    o_ref[...] = _dot3(_silu(c_ref[...]), wh_ref[...], wl_ref[...]) + b_ref[...]


def _ada(c, w_hi, w_lo, b):
    bsz, d = c.shape
    n = w_hi.shape[1] // d
    return pl.pallas_call(
        _ada_kernel,
        grid=(n,),
        in_specs=[pl.BlockSpec((bsz, d), lambda j: (0, 0)),
                  pl.BlockSpec((d, d), lambda j: (0, j)),
                  pl.BlockSpec((d, d), lambda j: (0, j)),
                  pl.BlockSpec((1, d), lambda j: (0, j))],
        out_specs=pl.BlockSpec((bsz, d), lambda j: (0, j)),
        out_shape=jax.ShapeDtypeStruct((bsz, n * d), F32),
        compiler_params=_cparams(1),
        name="ada",
    )(c, w_hi, w_lo, b)


INPROJ_TM = 512


def _inproj_kernel(x_ref, mod_ref, g_ref, w_ref,
                   qa_ref, ka_ref, va_ref, qb_ref, kb_ref, vb_ref, ga_ref, gb_ref):
    x = x_ref[0]
    h = _rmsnorm(x, g_ref[...]) * (1.0 + mod_ref[0, 1:2, :]) + mod_ref[0, 0:1, :]
    hb = h.astype(BF16)
    tm = x.shape[0]
    lo_half = lax.broadcasted_iota(I32, (tm, LANES), 1) < HEAD_DIM

    def proj(c0, n):
        return _dot(hb, w_ref[:, c0:c0 + n])

    def put_q(dst, base):
        for c in range(0, QA_W, 256):
            dst[0, :, c:c + 256] = (proj(base + c, 256) * Q_SCALE).astype(BF16)

    put_q(qa_ref, 0)
    r = proj(QA_W, 2 * KVA_W)
    for src, dst in ((r[:, :LANES], ka_ref), (r[:, LANES:], va_ref)):
        rolled = pltpu.roll(src, HEAD_DIM, axis=1)
        dst[0, :, 0 * LANES:1 * LANES] = jnp.where(lo_half, src, 0.0).astype(BF16)
        dst[0, :, 1 * LANES:2 * LANES] = jnp.where(lo_half, 0.0, rolled).astype(BF16)
        dst[0, :, 2 * LANES:3 * LANES] = jnp.where(lo_half, rolled, 0.0).astype(BF16)
        dst[0, :, 3 * LANES:4 * LANES] = jnp.where(lo_half, 0.0, src).astype(BF16)
    base_qb = QA_W + 2 * KVA_W
    put_q(qb_ref, base_qb)
    for dst, base in ((kb_ref, base_qb + QB_W), (vb_ref, base_qb + 2 * QB_W)):
        for c in range(0, QB_W, 256):
            r = proj(base + c, 256)
            for t in range(2):
                pair = r[:, t * LANES:(t + 1) * LANES]
                o = 2 * (c + t * LANES)
                dst[0, :, o:o + LANES] = jnp.where(lo_half, pair, 0.0).astype(BF16)
                dst[0, :, o + LANES:o + 2 * LANES] = jnp.where(lo_half, 0.0, pair).astype(BF16)
    base_g = base_qb + 3 * QB_W
    for dst, base in ((ga_ref, base_g), (gb_ref, base_g + D_MODEL)):
        for c in range(0, D_MODEL, 256):
            dst[0, :, c:c + 256] = jax.nn.sigmoid(proj(base + c, 256)).astype(BF16)


def _inproj(x, mod, g, w_in):
    bsz, s, d = x.shape
    tm = min(INPROJ_TM, s)
    widths = (QA_W, 4 * LANES, 4 * LANES, QB_W, 2 * QB_W, 2 * QB_W, d, d)
    return pl.pallas_call(
        _inproj_kernel,
        grid=(bsz, s // tm),
        in_specs=[pl.BlockSpec((1, tm, d), lambda b, i: (b, i, 0)),
                  pl.BlockSpec((1, N_MOD, d), lambda b, i: (b, 0, 0)),
                  pl.BlockSpec((1, d), lambda b, i: (0, 0)),
                  pl.BlockSpec((d, IN_WIDTH), lambda b, i: (0, 0))],
        out_specs=[pl.BlockSpec((1, tm, w), lambda b, i: (b, i, 0)) for w in widths],
        out_shape=[jax.ShapeDtypeStruct((bsz, s, w), BF16) for w in widths],
        compiler_params=_cparams(2),
        name="inproj",
    )(x, mod, g, w_in)


def _t5_buckets():
    i = np.arange(SWA_BLOCK)[:, None]
    j = np.arange(2 * SWA_BLOCK)[None, :]
    rel = (j - SWA_BLOCK) - i
    nb = NUM_BUCKETS // 2
    bucket = (rel > 0).astype(np.int32) * nb
    n = np.abs(rel)
    max_exact = nb // 2
    large = max_exact + (np.log(np.maximum(n, 1) / max_exact)
                         / np.log(MAX_DISTANCE / max_exact) * (nb - max_exact)).astype(np.int32)
    large = np.minimum(large, nb - 1)
    return (bucket + np.where(n < max_exact, n, large)).astype(np.int32)


def _swa_kernel(sink_ref, q_ref, kp_ref, kc_ref, vp_ref, vc_ref, bias_ref, o_ref):
    n = pl.program_id(1)
    shape = (SWA_BLOCK, 2 * SWA_BLOCK)
    row_hi = lax.broadcasted_iota(I32, shape, 0) // CHUNK
    col = lax.broadcasted_iota(I32, shape, 1)
    col_chunk = col // CHUNK
    valid = (col_chunk >= row_hi) & (col_chunk <= row_hi + WINDOW_CHUNKS)
    valid = valid & ((n > 0) | (col >= SWA_BLOCK))
    for p in range(SWA_HEADS // 2):
        q = q_ref[0, :, p * LANES:(p + 1) * LANES]
        acc = jnp.zeros((SWA_BLOCK, LANES), F32)
        for r in range(2):
            h = 2 * p + r
            slot = 2 * (h // (SWA_HEADS // SWA_KV_HEADS)) + r
            sl = slice(slot * LANES, (slot + 1) * LANES)
            kcat = jnp.concatenate([kp_ref[0, :, sl], kc_ref[0, :, sl]], axis=0)
            logits = jnp.where(valid, _dot_t(q, kcat) + bias_ref[h], NEG_INF)
            sink = sink_ref[h]
            m = jnp.maximum(jnp.max(logits, axis=-1, keepdims=True), sink)
            e = jnp.exp(logits - m)
            den = jnp.sum(e, axis=-1, keepdims=True) + jnp.exp(sink - m)
            probs = (e / den).astype(BF16)
            vcat = jnp.concatenate([vp_ref[0, :, sl], vc_ref[0, :, sl]], axis=0)
            acc = acc + _dot(probs, vcat)
        o_ref[0, :, p * LANES:(p + 1) * LANES] = acc.astype(BF16)


def _swa(qa, ka, va, bias, sinks):
    bsz, s, _ = qa.shape
    nb = s // SWA_BLOCK
    cur = lambda b, n: (b, n, 0)
    prev = lambda b, n: (b, jnp.maximum(n - 1, 0), 0)
    blk = (1, SWA_BLOCK, 4 * LANES)
    return pl.pallas_call(
        _swa_kernel,
        grid=(bsz, nb),
        in_specs=[pl.BlockSpec(memory_space=pltpu.SMEM),
                  pl.BlockSpec(blk, cur),
                  pl.BlockSpec(blk, prev), pl.BlockSpec(blk, cur),
                  pl.BlockSpec(blk, prev), pl.BlockSpec(blk, cur),
                  pl.BlockSpec((SWA_HEADS, SWA_BLOCK, 2 * SWA_BLOCK), lambda b, n: (0, 0, 0))],
        out_specs=pl.BlockSpec(blk, cur),
        out_shape=jax.ShapeDtypeStruct((bsz, s, QA_W), BF16),
        compiler_params=_cparams(2),
        name="swa",
    )(sinks, qa, ka, ka, va, va, bias)


def _sb_kernel(q_ref, k_ref, v_ref, tri_ref, o_ref):
    i = pl.program_id(2)
    q = q_ref[0]
    tri = tri_ref[...]
    shape = (SB_BLOCK, 2 * LANES)
    row = lax.broadcasted_iota(I32, shape, 0)
    col = lax.broadcasted_iota(I32, shape, 1) % LANES
    causal = col < row

    def stack(ref, r0):
        t = ref[0, pl.ds(r0, SB_BLOCK), :]
        return jnp.concatenate([t[:, :LANES], t[:, LANES:]], axis=0)

    def tile(kt, carry, diag):
        acc, ca, cb = carry
        r0 = pl.multiple_of(kt * SB_BLOCK, SB_BLOCK)
        z = _dot_t(q, stack(k_ref, r0))
        soft = jnp.log(1.0 + jnp.exp(-jnp.abs(z)))
        lk = -(jnp.maximum(z, 0.0) + soft)
        ls = lk + z
        if diag:
            lk = jnp.where(causal, lk, 0.0)
        hi, lo = _split(lk)
        rest = _dot(hi, tri) + _dot(lo, tri)
        cfull = jnp.concatenate([jnp.broadcast_to(ca, (SB_BLOCK, LANES)),
                                 jnp.broadcast_to(cb, (SB_BLOCK, LANES))], axis=1)
        w = jnp.exp(ls + rest + cfull)
        if diag:
            w = jnp.where(causal, w, 0.0)
        acc = acc + _dot(w.astype(BF16), stack(v_ref, r0))
        ca = ca + jnp.sum(lk[:, :LANES], axis=-1, keepdims=True)
        cb = cb + jnp.sum(lk[:, LANES:], axis=-1, keepdims=True)
        return acc, ca, cb

    zero_col = jnp.zeros((SB_BLOCK, 1), F32)
    carry = tile(i, (jnp.zeros((SB_BLOCK, LANES), F32), zero_col, zero_col), True)
    carry = lax.fori_loop(0, i, lambda j, c: tile(i - 1 - j, c, False), carry)
    o_ref[0] = carry[0].astype(BF16)


def _sb(qb, kb, vb, tri):
    bsz, s, _ = qb.shape
    nq = s // SB_BLOCK
    return pl.pallas_call(
        _sb_kernel,
        grid=(bsz, HEAD_PAIRS, nq),
        in_specs=[pl.BlockSpec((1, SB_BLOCK, LANES), lambda b, p, i: (b, i, p)),
                  pl.BlockSpec((1, s, 2 * LANES), lambda b, p, i: (b, 0, p)),
                  pl.BlockSpec((1, s, 2 * LANES), lambda b, p, i: (b, 0, p)),
                  pl.BlockSpec((2 * LANES, 2 * LANES), lambda b, p, i: (0, 0))],
        out_specs=pl.BlockSpec((1, SB_BLOCK, LANES), lambda b, p, i: (b, i, p)),
        out_shape=jax.ShapeDtypeStruct((bsz, s, QB_W), BF16),
        compiler_params=_cparams(3),
        name="sb",
    )(qb, kb, vb, tri)


MIX_TM = 256


def _seg_allreduce(v, lane, op):
    n = v.shape[1]
    sh = 1
    while sh < GROUP_SIZE:
        up = pltpu.roll(v, n - sh, axis=1)
        dn = pltpu.roll(v, sh, axis=1)
        v = op(v, jnp.where((lane & sh) == 0, up, dn))
        sh *= 2
    return v


def _route(scores, bias, lane):
    neg = -jnp.inf
    choice = scores + bias
    m1 = _seg_allreduce(choice, lane, jnp.maximum)
    is_top = choice == m1
    n_top = _seg_allreduce(is_top.astype(F32), lane, jnp.add)
    m2 = _seg_allreduce(jnp.where(is_top, neg, choice), lane, jnp.maximum)
    gscore = m1 + jnp.where(n_top >= 2.0, m1, m2)
    grp = lane // GROUP_SIZE
    n_lanes = scores.shape[1]
    rank = jnp.zeros(scores.shape, F32)
    for d in range(1, N_GROUPS):
        other = pltpu.roll(gscore, GROUP_SIZE * d, axis=1)
        beats = (other > gscore) | ((other == gscore) & (grp >= d))
        rank = rank + beats.astype(F32)
    masked = jnp.where(rank < float(TOPK_GROUPS), choice, neg)
    lane_f = lane.astype(F32)
    idx_cols, w_cols = [], []
    sel = jnp.zeros(scores.shape, jnp.bool_)
    for _ in range(TOP_K):
        m = jnp.max(masked, axis=-1, keepdims=True)
        first = jnp.min(jnp.where(masked == m, lane_f, float(n_lanes)), axis=-1, keepdims=True)
        hit = lane_f == first
        idx_cols.append(first)
        w_cols.append(jnp.sum(jnp.where(hit, scores, 0.0), axis=-1, keepdims=True))
        masked = jnp.where(hit, neg, masked)
        sel = sel | hit
    return idx_cols, w_cols, sel


def _cols_to_block(cols, k_iota):
    out = jnp.zeros(k_iota.shape, cols[0].dtype)
    for k, c in enumerate(cols):
        out = jnp.where(k_iota == k, c, out)
    return out


def _mix_kernel(ya_ref, yb_ref, ga_ref, gb_ref, x_ref, mod_ref, wa_ref, wb_ref, wo_ref,
                gpm_ref, gpf_ref, wrh_ref, wrl_ref, rb_ref,
                x1_ref, h2_ref, idx_ref, wt_ref, cnt_ref):
    first = (pl.program_id(0) == 0) & (pl.program_id(1) == 0)
    merged = (ga_ref[0].astype(F32) * _dot(ya_ref[0], wa_ref[...])
              + gb_ref[0].astype(F32) * _dot(yb_ref[0], wb_ref[...]))
    o = _dot(merged.astype(BF16), wo_ref[...])
    x1 = x_ref[0] + mod_ref[0, 2:3, :] * _rmsnorm(o, gpm_ref[...])
    h2 = _rmsnorm(x1, gpf_ref[...]) * (1.0 + mod_ref[0, 4:5, :]) + mod_ref[0, 3:4, :]
    x1_ref[0] = x1
    h2_ref[0] = h2
    scores = jax.nn.sigmoid(_dot3(h2, wrh_ref[...], wrl_ref[...]))
    tm = scores.shape[0]
    lane = lax.broadcasted_iota(I32, (tm, N_EXPERTS), 1)
    idx_cols, w_cols, sel = _route(scores, rb_ref[...], lane)
    wsum = w_cols[0]
    for c in w_cols[1:]:
        wsum = wsum + c
    k_iota = lax.broadcasted_iota(I32, (tm, TOP_K), 1)
    idx_ref[0] = _cols_to_block([c.astype(I32) for c in idx_cols], k_iota)
    wt_ref[0] = _cols_to_block([c / wsum * ROUTED_SCALE for c in w_cols], k_iota)

    @pl.when(first)
    def _():
        cnt_ref[...] = jnp.zeros_like(cnt_ref)

    cnt_ref[...] += jnp.sum(sel.astype(F32), axis=0, keepdims=True)


def _mix(ya, yb, ga, gb, x, mod, wa, wb, wo, gpm, gpf, wr_hi, wr_lo, rbias):
    bsz, s, d = x.shape
    tm = min(MIX_TM, s)
    tok = lambda b, i: (b, i, 0)
    const = lambda b, i: (0, 0)
    return pl.pallas_call(
        _mix_kernel,
        grid=(bsz, s // tm),
        in_specs=[pl.BlockSpec((1, tm, QA_W), tok), pl.BlockSpec((1, tm, QB_W), tok),
                  pl.BlockSpec((1, tm, d), tok), pl.BlockSpec((1, tm, d), tok),
                  pl.BlockSpec((1, tm, d), tok),
                  pl.BlockSpec((1, N_MOD, d), lambda b, i: (b, 0, 0)),
                  pl.BlockSpec((QA_W, d), const), pl.BlockSpec((QB_W, d), const),
                  pl.BlockSpec((d, d), const),
                  pl.BlockSpec((1, d), const), pl.BlockSpec((1, d), const),
                  pl.BlockSpec((d, N_EXPERTS), const), pl.BlockSpec((d, N_EXPERTS), const),
                  pl.BlockSpec((1, N_EXPERTS), const)],
        out_specs=[pl.BlockSpec((1, tm, d), tok), pl.BlockSpec((1, tm, d), tok),
                   pl.BlockSpec((1, tm, TOP_K), tok), pl.BlockSpec((1, tm, TOP_K), tok),
                   pl.BlockSpec((1, N_EXPERTS), const)],
        out_shape=[jax.ShapeDtypeStruct((bsz, s, d), F32), jax.ShapeDtypeStruct((bsz, s, d), F32),
                   jax.ShapeDtypeStruct((bsz, s, TOP_K), I32), jax.ShapeDtypeStruct((bsz, s, TOP_K), F32),
                   jax.ShapeDtypeStruct((1, N_EXPERTS), F32)],
        compiler_params=_cparams(2),
        name="mix",
    )(ya, yb, ga, gb, x, mod, wa, wb, wo, gpm, gpf, wr_hi, wr_lo, rbias)


POS_TM = 256


def _pos_kernel(idx_ref, start_ref, tri_ref, pos_ref, carry_ref):
    @pl.when(pl.program_id(0) == 0)
    def _():
        carry_ref[...] = jnp.zeros_like(carry_ref)

    idx = idx_ref[...]
    tm = idx.shape[0]
    lane = lax.broadcasted_iota(I32, (tm, N_EXPERTS), 1)
    hits = [lane == idx[:, k:k + 1] for k in range(TOP_K)]
    sel = hits[0]
    for h in hits[1:]:
        sel = sel | h
    dense = _dot(tri_ref[...], sel.astype(BF16)) + carry_ref[...] + start_ref[...]
    cols = [jnp.sum(jnp.where(h, dense, 0.0), axis=-1, keepdims=True).astype(I32) for h in hits]
    pos_ref[...] = _cols_to_block(cols, lax.broadcasted_iota(I32, (tm, TOP_K), 1))
    carry_ref[...] += jnp.sum(sel.astype(F32), axis=0, keepdims=True)


def _positions(idx, start, tri):
    t = idx.shape[0]
    tm = min(POS_TM, t)
    return pl.pallas_call(
        _pos_kernel,
        grid=(t // tm,),
        in_specs=[pl.BlockSpec((tm, TOP_K), lambda i: (i, 0)),
                  pl.BlockSpec((1, N_EXPERTS), lambda i: (0, 0)),
                  pl.BlockSpec((tm, tm), lambda i: (0, 0))],
        out_specs=pl.BlockSpec((tm, TOP_K), lambda i: (i, 0)),
        out_shape=jax.ShapeDtypeStruct((t, TOP_K), I32),
        scratch_shapes=[pltpu.VMEM((1, N_EXPERTS), F32)],
        compiler_params=_cparams(1),
        name="pos",
    )(idx, start, tri)


def _experts_kernel(be_ref, nu_ref, tok_ref, h_ref, wg_ref, wu_ref, wd_ref, y_ref,
                    idx_smem, xbuf, wg_b, wu_b, wd_b, sem_i, sem_x):
    i = pl.program_id(0)
    n_used = nu_ref[0]
    slot = i % 2

    def idx_copy(j, s):
        return pltpu.make_async_copy(tok_ref.at[j], idx_smem.at[s], sem_i.at[s])

    def issue_rows(s):
        def body(r, c):
            tok = idx_smem[s, r]
            pltpu.make_async_copy(h_ref.at[pl.ds(tok, 1), :], xbuf.at[s, pl.ds(r, 1), :],
                                  sem_x.at[s]).start()
            return c
        lax.fori_loop(0, MOE_BLOCK, body, 0, unroll=8)

    def wait_rows(s):
        pltpu.make_async_copy(h_ref.at[pl.ds(0, MOE_BLOCK), :], xbuf.at[s], sem_x.at[s]).wait()

    @pl.when((i == 0) & (n_used > 0))
    def _():
        idx_copy(0, 0).start()
        idx_copy(0, 0).wait()
        issue_rows(0)

        @pl.when(n_used > 1)
        def _():
            idx_copy(1, 1).start()

    @pl.when(i + 1 < n_used)
    def _():
        idx_copy(i + 1, 1 - slot).wait()
        issue_rows(1 - slot)

    @pl.when(i + 2 < n_used)
    def _():
        idx_copy(i + 2, slot).start()

    @pl.when(i < n_used)
    def _():
        @pl.when((i == 0) | (be_ref[i] != be_ref[jnp.maximum(i - 1, 0)]))
        def _():
            wg_b[...] = wg_ref[0].astype(BF16)
            wu_b[...] = wu_ref[0].astype(BF16)
            wd_b[...] = wd_ref[0].astype(BF16)

        wait_rows(slot)
        xb = xbuf[slot].astype(BF16)
        hid = _silu(_dot(xb, wg_b[...])) * _dot(xb, wu_b[...])
        y_ref[...] = _dot(hid.astype(BF16), wd_b[...])

    @pl.when(i >= n_used)
    def _():
        y_ref[...] = jnp.zeros_like(y_ref)


def _experts(block_e, n_used, tok_pad, h2, wg, wu, wd):
    n_blocks = tok_pad.shape[0]
    d = h2.shape[1]
    e_dim = wg.shape[2]
    grid_spec = pltpu.PrefetchScalarGridSpec(
        num_scalar_prefetch=2,
        grid=(n_blocks,),
        in_specs=[pl.BlockSpec((n_blocks, MOE_BLOCK), lambda i, be, nu: (0, 0)),
                  pl.BlockSpec(memory_space=pl.ANY),
                  pl.BlockSpec((1, d, e_dim), lambda i, be, nu: (be[i], 0, 0)),
                  pl.BlockSpec((1, d, e_dim), lambda i, be, nu: (be[i], 0, 0)),
                  pl.BlockSpec((1, e_dim, d), lambda i, be, nu: (be[i], 0, 0))],
        out_specs=pl.BlockSpec((MOE_BLOCK, d), lambda i, be, nu: (i, 0)),
        scratch_shapes=[pltpu.SMEM((2, MOE_BLOCK), I32),
                        pltpu.VMEM((2, MOE_BLOCK, d), F32),
                        pltpu.VMEM((d, e_dim), BF16), pltpu.VMEM((d, e_dim), BF16),
                        pltpu.VMEM((e_dim, d), BF16),
                        pltpu.SemaphoreType.DMA((2,)), pltpu.SemaphoreType.DMA((2,))],
    )
    return pl.pallas_call(
        _experts_kernel,
        grid_spec=grid_spec,
        out_shape=jax.ShapeDtypeStruct((n_blocks * MOE_BLOCK, d), F32),
        compiler_params=_cparams(1),
        name="experts",
    )(block_e, n_used, tok_pad, h2, wg, wu, wd)


FIN_TM = 128


def _final_kernel(pos_ref, ys_ref, wt_ref, h2_ref, x1_ref, mod_ref, wgs_ref, wus_ref, wds_ref, g_ref,
                  o_ref, idx_smem, gbuf, sem_i, sem_g):
    b, i = pl.program_id(0), pl.program_id(1)
    nt = pl.num_programs(1)
    step = b * nt + i
    n_steps = pl.num_programs(0) * nt
    slot = step % 2
    tm = o_ref.shape[1]

    def idx_copy(j, s):
        return pltpu.make_async_copy(pos_ref.at[j], idx_smem.at[s], sem_i.at[s])

    def issue_rows(s):
        for k in range(TOP_K):
            def body(r, c, k=k):
                row = idx_smem[s, k * tm + r]
                pltpu.make_async_copy(ys_ref.at[pl.ds(row, 1), :], gbuf.at[s, k, pl.ds(r, 1), :],
                                      sem_g.at[s]).start()
                return c
            lax.fori_loop(0, tm, body, 0, unroll=8)

    def wait_rows(s):
        for k in range(TOP_K):
            pltpu.make_async_copy(ys_ref.at[pl.ds(0, tm), :], gbuf.at[s, k], sem_g.at[s]).wait()

    @pl.when(step == 0)
    def _():
        idx_copy(0, 0).start()
        idx_copy(0, 0).wait()
        issue_rows(0)

        @pl.when(n_steps > 1)
        def _():
            idx_copy(1, 1).start()

    @pl.when(step + 1 < n_steps)
    def _():
        idx_copy(step + 1, 1 - slot).wait()
        issue_rows(1 - slot)

    @pl.when(step + 2 < n_steps)
    def _():
        idx_copy(step + 2, slot).start()

    hb = h2_ref[0].astype(BF16)
    shared = _dot((_silu(_dot(hb, wgs_ref[...])) * _dot(hb, wus_ref[...])).astype(BF16), wds_ref[...])
    wait_rows(slot)
    wt = wt_ref[0]
    y = wt[:, 0:1] * gbuf[slot, 0]
    for k in range(1, TOP_K):
        y = y + wt[:, k:k + 1] * gbuf[slot, k]
    y = y + shared
    o_ref[0] = x1_ref[0] + mod_ref[0, 5:6, :] * _rmsnorm(y, g_ref[...])


def _final(pos_tiles, ys, wts, h2, x1, mod, wgs, wus, wds, g):
    bsz, s, d = x1.shape
    tm = min(FIN_TM, s)
    nt = s // tm
    n_tiles = bsz * nt
    sd = wgs.shape[1]
    tok = lambda b, i: (b, i, 0)
    const = lambda b, i: (0, 0)
    return pl.pallas_call(
        _final_kernel,
        grid=(bsz, nt),
        in_specs=[pl.BlockSpec((n_tiles, TOP_K * tm), const),
                  pl.BlockSpec(memory_space=pl.ANY),
                  pl.BlockSpec((1, tm, TOP_K), tok),
                  pl.BlockSpec((1, tm, d), tok), pl.BlockSpec((1, tm, d), tok),
                  pl.BlockSpec((1, N_MOD, d), lambda b, i: (b, 0, 0)),
                  pl.BlockSpec((d, sd), const), pl.BlockSpec((d, sd), const), pl.BlockSpec((sd, d), const),
                  pl.BlockSpec((1, d), const)],
        out_specs=pl.BlockSpec((1, tm, d), tok),
        out_shape=jax.ShapeDtypeStruct((bsz, s, d), F32),
        scratch_shapes=[pltpu.SMEM((2, TOP_K * tm), I32),
                        pltpu.VMEM((2, TOP_K, tm, d), F32),
                        pltpu.SemaphoreType.DMA((2,)), pltpu.SemaphoreType.DMA((2,))],
        compiler_params=_cparams(2),
        name="final",
    )(pos_tiles, ys, wts, h2, x1, mod, wgs, wus, wds, g)


def _strict_lower(n):
    r = np.arange(n)
    return r[None, :] < r[:, None]


def kernel(x, c, w_ada, b_ada, g_pre_mix, g_post_mix, w_in, attn_sinks, rel_bias, w_branch_a, w_branch_b,
           w_out, g_pre_ffn, g_post_ffn, w_router, router_bias, w_gate_e, w_up_e, w_down_e,
           w_gate_s, w_up_s, w_down_s):
    bsz, s, d = x.shape
    t = bsz * s
    depth = w_ada.shape[0]
    bias_tab = rel_bias.astype(F32)[_t5_buckets()].transpose(2, 0, 1)
    sb_tri = np.kron(np.eye(2), _strict_lower(LANES).astype(np.float32))
    sb_tri = jnp.asarray(sb_tri, BF16)
    pos_tri = jnp.asarray(_strict_lower(min(POS_TM, t)), BF16)
    for l in range(depth):
        wa_hi, wa_lo = _split(w_ada[l])
        mod = _ada(c, wa_hi, wa_lo, b_ada[l][None, :]).reshape(bsz, N_MOD, d)
        qa, ka, va, qb, kb, vb, ga, gb = _inproj(x, mod, g_pre_mix[l][None, :], w_in[l].astype(BF16))
        ya = _swa(qa, ka, va, bias_tab, attn_sinks[l])
        yb = _sb(qb, kb, vb, sb_tri)
        wr_hi, wr_lo = _split(w_router[l])
        x1, h2, idx, wts, counts = _mix(
            ya, yb, ga, gb, x, mod, w_branch_a[l].astype(BF16), w_branch_b[l].astype(BF16),
            w_out[l].astype(BF16), g_post_mix[l][None, :], g_pre_ffn[l][None, :],
            wr_hi, wr_lo, router_bias[l][None, :])
        counts = counts[0].astype(I32)
        pad_counts = (counts + MOE_BLOCK - 1) // MOE_BLOCK * MOE_BLOCK
        pad_end = jnp.cumsum(pad_counts)
        pad_start = pad_end - pad_counts
        n_blocks = -(-(t * TOP_K) // MOE_BLOCK) + N_EXPERTS
        block_e = jnp.minimum(
            jnp.searchsorted(pad_end, jnp.arange(n_blocks, dtype=I32) * MOE_BLOCK, side='right'),
            N_EXPERTS - 1).astype(I32)
        n_used = (pad_end[-1:] // MOE_BLOCK).astype(I32)
        pos = _positions(idx.reshape(t, TOP_K), pad_start.astype(F32)[None, :], pos_tri)
        tok_flat = jnp.arange(t * TOP_K, dtype=I32) // TOP_K
        tok_pad = jnp.zeros((n_blocks * MOE_BLOCK,), I32).at[pos.reshape(-1)].set(
            tok_flat, unique_indices=True).reshape(n_blocks, MOE_BLOCK)
        ys = _experts(block_e, n_used, tok_pad, h2.reshape(t, d), w_gate_e[l], w_up_e[l], w_down_e[l])
        tm = min(FIN_TM, s)
        pos_tiles = pos.reshape(t // tm, tm, TOP_K).transpose(0, 2, 1).reshape(t // tm, TOP_K * tm)
        x = _final(pos_tiles, ys, wts, h2, x1, mod, w_gate_s[l].astype(BF16), w_up_s[l].astype(BF16),
                   w_down_s[l].astype(BF16), g_post_ffn[l][None, :])
    return x
```

```python
import functools

import numpy as np
import jax
import jax.numpy as jnp
from jax import lax
from jax.experimental import pallas as pl
from jax.experimental.pallas import tpu as pltpu

F32 = jnp.float32
BF16 = jnp.bfloat16
I32 = jnp.int32

D_MODEL = 1024
CHUNK = 64
HEAD_DIM = 64
SWA_HEADS = 8
SWA_KV_HEADS = 2
SWA_BLOCK = 128
WINDOW_CHUNKS = 2
SB_HEADS = 8
SB_BLOCK = 128
NUM_BUCKETS = 32
MAX_DISTANCE = 128
N_EXPERTS = 256
TOP_K = 8
N_GROUPS = 8
GROUP_SIZE = N_EXPERTS // N_GROUPS
TOPK_GROUPS = 4
EXPERT_DIM = 256
ROUTED_SCALE = 2.5
MOE_BLOCK = 128
RMS_EPS = 1e-6
N_MOD = 6
NEG_INF = -1e30

QA_W = SWA_HEADS * HEAD_DIM
KVA_W = SWA_KV_HEADS * HEAD_DIM
QB_W = SB_HEADS * HEAD_DIM
IN_WIDTH = QA_W + 2 * KVA_W + 3 * QB_W + 2 * D_MODEL
LANES = 128
HEAD_PAIRS = SB_HEADS // 2
Q_SCALE = HEAD_DIM ** -0.5
LOG2_E = 1.4426950408889634

VMEM_LIMIT = 56 * 1024 * 1024


def _cparams(n_axes, vmem=VMEM_LIMIT):
    return pltpu.CompilerParams(dimension_semantics=("arbitrary",) * n_axes, vmem_limit_bytes=vmem)


def _dot(a, b):
    return jnp.dot(a, b, preferred_element_type=F32)


def _dot_t(a, b):
    return lax.dot_general(a, b, (((1,), (1,)), ((), ())), preferred_element_type=F32)


def _split(x):
    hi = x.astype(BF16)
    lo = (x - hi.astype(F32)).astype(BF16)
    return hi, lo


def _dot3(a, b_hi, b_lo):
    a_hi, a_lo = _split(a)
    return _dot(a_hi, b_hi) + (_dot(a_hi, b_lo) + _dot(a_lo, b_hi))


def _rmsnorm(x, g):
    return x * lax.rsqrt(jnp.mean(x * x, axis=-1, keepdims=True) + RMS_EPS) * g


def _silu(x):
    return x * jax.nn.sigmoid(x)


ROW_TILE = D_MODEL // LANES


def _store_row_tiles(ref, x, n):
    for c in range(ROW_TILE):
        ref[pl.ds(c, n, stride=ROW_TILE), :] = x[:, c * LANES:(c + 1) * LANES]


def _load_row_tiles(ref, n):
    return jnp.concatenate([ref[pl.ds(c, n, stride=ROW_TILE), :] for c in range(ROW_TILE)], axis=1)


def _ada_kernel(c_ref, wh_ref, wl_ref, b_ref, o_ref):
    o_ref[...] = _dot3(_silu(c_ref[...]), wh_ref[...], wl_ref[...]) + b_ref[...]


def _ada(c, w_hi, w_lo, b):
    bsz, d = c.shape
    n = w_hi.shape[1] // d
    return pl.pallas_call(
        _ada_kernel,
        grid=(n,),
        in_specs=[pl.BlockSpec((bsz, d), lambda j: (0, 0)),
                  pl.BlockSpec((d, d), lambda j: (0, j)),
                  pl.BlockSpec((d, d), lambda j: (0, j)),
                  pl.BlockSpec((1, d), lambda j: (0, j))],
        out_specs=pl.BlockSpec((bsz, d), lambda j: (0, j)),
        out_shape=jax.ShapeDtypeStruct((bsz, n * d), F32),
        compiler_params=_cparams(1),
        name="ada",
    )(c, w_hi, w_lo, b)


INPROJ_TM = 512


def _inproj_kernel(x_ref, mod_ref, g_ref, w_ref,
                   qa_ref, ka_ref, va_ref, qb_ref, kb_ref, vb_ref, ga_ref, gb_ref):
    x = x_ref[0]
    h = _rmsnorm(x, g_ref[...]) * (1.0 + mod_ref[0, 1:2, :]) + mod_ref[0, 0:1, :]
    hb = h.astype(BF16)
    tm = x.shape[0]
    lo_half = lax.broadcasted_iota(I32, (tm, LANES), 1) < HEAD_DIM

    def proj(c0, n):
        return _dot(hb, w_ref[:, c0:c0 + n])

    def put_q(dst, base, scale):
        for c in range(0, QA_W, 256):
            dst[0, :, c:c + 256] = (proj(base + c, 256) * scale).astype(BF16)

    put_q(qa_ref, 0, Q_SCALE)
    r = proj(QA_W, 2 * KVA_W)
    for src, dst in ((r[:, :LANES], ka_ref), (r[:, LANES:], va_ref)):
        rolled = pltpu.roll(src, HEAD_DIM, axis=1)
        dst[0, :, 0 * LANES:1 * LANES] = jnp.where(lo_half, src, 0.0).astype(BF16)
        dst[0, :, 1 * LANES:2 * LANES] = jnp.where(lo_half, 0.0, rolled).astype(BF16)
        dst[0, :, 2 * LANES:3 * LANES] = jnp.where(lo_half, rolled, 0.0).astype(BF16)
        dst[0, :, 3 * LANES:4 * LANES] = jnp.where(lo_half, 0.0, src).astype(BF16)
    base_qb = QA_W + 2 * KVA_W
    put_q(qb_ref, base_qb, Q_SCALE * LOG2_E)
    for dst, base in ((kb_ref, base_qb + QB_W), (vb_ref, base_qb + 2 * QB_W)):
        for c in range(0, QB_W, 256):
            r = proj(base + c, 256)
            for t in range(2):
                pair = r[:, t * LANES:(t + 1) * LANES]
                o = 2 * (c + t * LANES)
                dst[0, :, o:o + LANES] = jnp.where(lo_half, pair, 0.0).astype(BF16)
                dst[0, :, o + LANES:o + 2 * LANES] = jnp.where(lo_half, 0.0, pair).astype(BF16)
    base_g = base_qb + 3 * QB_W
    for dst, base in ((ga_ref, base_g), (gb_ref, base_g + D_MODEL)):
        for c in range(0, D_MODEL, 256):
            dst[0, :, c:c + 256] = jax.nn.sigmoid(proj(base + c, 256)).astype(BF16)


def _inproj(x, mod, g, w_in):
    bsz, s, d = x.shape
    tm = min(INPROJ_TM, s)
    widths = (QA_W, 4 * LANES, 4 * LANES, QB_W, 2 * QB_W, 2 * QB_W, d, d)
    return pl.pallas_call(
        _inproj_kernel,
        grid=(bsz, s // tm),
        in_specs=[pl.BlockSpec((1, tm, d), lambda b, i: (b, i, 0)),
                  pl.BlockSpec((1, N_MOD, d), lambda b, i: (b, 0, 0)),
                  pl.BlockSpec((1, d), lambda b, i: (0, 0)),
                  pl.BlockSpec((d, IN_WIDTH), lambda b, i: (0, 0))],
        out_specs=[pl.BlockSpec((1, tm, w), lambda b, i: (b, i, 0)) for w in widths],
        out_shape=[jax.ShapeDtypeStruct((bsz, s, w), BF16) for w in widths],
        compiler_params=_cparams(2),
        name="inproj",
    )(x, mod, g, w_in)


def _t5_buckets():
    i = np.arange(SWA_BLOCK)[:, None]
    j = np.arange(2 * SWA_BLOCK)[None, :]
    rel = (j - SWA_BLOCK) - i
    nb = NUM_BUCKETS // 2
    bucket = (rel > 0).astype(np.int32) * nb
    n = np.abs(rel)
    max_exact = nb // 2
    large = max_exact + (np.log(np.maximum(n, 1) / max_exact)
                         / np.log(MAX_DISTANCE / max_exact) * (nb - max_exact)).astype(np.int32)
    large = np.minimum(large, nb - 1)
    return (bucket + np.where(n < max_exact, n, large)).astype(np.int32)


def _swa_kernel(sink_ref, q_ref, kp_ref, kc_ref, vp_ref, vc_ref, bias_ref, o_ref):
    n = pl.program_id(1)
    shape = (SWA_BLOCK, 2 * SWA_BLOCK)
    row_hi = lax.broadcasted_iota(I32, shape, 0) // CHUNK
    col = lax.broadcasted_iota(I32, shape, 1)
    col_chunk = col // CHUNK
    valid = (col_chunk >= row_hi) & (col_chunk <= row_hi + WINDOW_CHUNKS)
    valid = valid & ((n > 0) | (col >= SWA_BLOCK))
    for p in range(SWA_HEADS // 2):
        q = q_ref[0, :, p * LANES:(p + 1) * LANES]
        acc = jnp.zeros((SWA_BLOCK, LANES), F32)
        for r in range(2):
            h = 2 * p + r
            slot = 2 * (h // (SWA_HEADS // SWA_KV_HEADS)) + r
            sl = slice(slot * LANES, (slot + 1) * LANES)
            kcat = jnp.concatenate([kp_ref[0, :, sl], kc_ref[0, :, sl]], axis=0)
            logits = jnp.where(valid, _dot_t(q, kcat) + bias_ref[h], NEG_INF)
            sink = sink_ref[h]
            m = jnp.maximum(jnp.max(logits, axis=-1, keepdims=True), sink)
            e = jnp.exp(logits - m)
            den = jnp.sum(e, axis=-1, keepdims=True) + jnp.exp(sink - m)
            probs = (e / den).astype(BF16)
            vcat = jnp.concatenate([vp_ref[0, :, sl], vc_ref[0, :, sl]], axis=0)
            acc = acc + _dot(probs, vcat)
        o_ref[0, :, p * LANES:(p + 1) * LANES] = acc.astype(BF16)


def _swa(qa, ka, va, bias, sinks):
    bsz, s, _ = qa.shape
    nb = s // SWA_BLOCK
    cur = lambda b, n: (b, n, 0)
    prev = lambda b, n: (b, jnp.maximum(n - 1, 0), 0)
    blk = (1, SWA_BLOCK, 4 * LANES)
    return pl.pallas_call(
        _swa_kernel,
        grid=(bsz, nb),
        in_specs=[pl.BlockSpec(memory_space=pltpu.SMEM),
                  pl.BlockSpec(blk, cur),
                  pl.BlockSpec(blk, prev), pl.BlockSpec(blk, cur),
                  pl.BlockSpec(blk, prev), pl.BlockSpec(blk, cur),
                  pl.BlockSpec((SWA_HEADS, SWA_BLOCK, 2 * SWA_BLOCK), lambda b, n: (0, 0, 0))],
        out_specs=pl.BlockSpec(blk, cur),
        out_shape=jax.ShapeDtypeStruct((bsz, s, QA_W), BF16),
        compiler_params=_cparams(2),
        name="swa",
    )(sinks, qa, ka, ka, va, va, bias)


SB_T = 2 * SB_BLOCK


def _sb_kernel(q_ref, k_ref, v_ref, tri_ref, o_ref, acc_ref, c_ref):
    i = pl.program_id(1)
    tri = tri_ref[...]
    shape = (SB_T, SB_T)
    causal = lax.broadcasted_iota(I32, shape, 1) < lax.broadcasted_iota(I32, shape, 0)
    acc_ref[...] = jnp.zeros_like(acc_ref)
    c_ref[...] = jnp.zeros_like(c_ref)

    def stack(ref, r0, p):
        t = ref[0, pl.ds(r0, SB_T), 2 * p * LANES:(2 * p + 2) * LANES]
        return jnp.concatenate([t[:, :LANES], t[:, LANES:]], axis=0)

    def sweep(j, diag):
        r0 = pl.multiple_of(j * SB_T, SB_T)
        for p in range(HEAD_PAIRS):
            q = q_ref[0, :, p * LANES:(p + 1) * LANES]
            z2 = _dot_t(q, stack(k_ref, r0, p))
            ws = []
            for r in range(2):
                h = 2 * p + r
                z = z2[:, r * SB_T:(r + 1) * SB_T]
                nlk = jnp.maximum(z, 0.0) + jnp.log2(1.0 + jnp.exp2(-jnp.abs(z)))
                if diag:
                    nlk = jnp.where(causal, nlk, 0.0)
                hi, lo = _split(nlk)
                rr = _dot(jnp.concatenate([hi, lo], axis=0), tri)
                c = c_ref[h]
                lw = z - nlk - (rr[:SB_T] + rr[SB_T:])
                w = jnp.concatenate([jnp.exp2(lw[:, :LANES] - c), jnp.exp2(lw[:, LANES:] - c)], axis=1)
                if diag:
                    w = jnp.where(causal, w, 0.0)
                ws.append(w.astype(BF16))
                c_ref[h] = c + jnp.sum(nlk, axis=-1, keepdims=True)
            acc_ref[p] += _dot(jnp.concatenate(ws, axis=1), stack(v_ref, r0, p))

    sweep(i, True)

    def body(t, carry):
        sweep(i - 1 - t, False)
        return carry

    lax.fori_loop(0, i, body, 0)
    for p in range(HEAD_PAIRS):
        o_ref[0, :, p * LANES:(p + 1) * LANES] = acc_ref[p].astype(BF16)


def _sb(qb, kb, vb, tri):
    bsz, s, _ = qb.shape
    return pl.pallas_call(
        _sb_kernel,
        grid=(bsz, s // SB_T),
        in_specs=[pl.BlockSpec((1, SB_T, QB_W), lambda b, i: (b, i, 0)),
                  pl.BlockSpec((1, s, 2 * QB_W), lambda b, i: (b, 0, 0)),
                  pl.BlockSpec((1, s, 2 * QB_W), lambda b, i: (b, 0, 0)),
                  pl.BlockSpec((SB_T, SB_T), lambda b, i: (0, 0))],
        out_specs=pl.BlockSpec((1, SB_T, QB_W), lambda b, i: (b, i, 0)),
        out_shape=jax.ShapeDtypeStruct((bsz, s, QB_W), BF16),
        scratch_shapes=[pltpu.VMEM((HEAD_PAIRS, SB_T, LANES), F32),
                        pltpu.VMEM((SB_HEADS, SB_T, LANES), F32)],
        compiler_params=_cparams(2),
        name="sb",
    )(qb, kb, vb, tri)


MIX_TM = 256


def _seg_allreduce(v, lane, op):
    n = v.shape[1]
    sh = 1
    while sh < GROUP_SIZE:
        up = pltpu.roll(v, n - sh, axis=1)
        dn = pltpu.roll(v, sh, axis=1)
        v = op(v, jnp.where((lane & sh) == 0, up, dn))
        sh *= 2
    return v


def _route(scores, bias, lane):
    neg = -jnp.inf
    choice = scores + bias
    m1 = _seg_allreduce(choice, lane, jnp.maximum)
    is_top = choice == m1
    n_top = _seg_allreduce(is_top.astype(F32), lane, jnp.add)
    m2 = _seg_allreduce(jnp.where(is_top, neg, choice), lane, jnp.maximum)
    gscore = m1 + jnp.where(n_top >= 2.0, m1, m2)
    grp = lane // GROUP_SIZE
    n_lanes = scores.shape[1]
    rank = jnp.zeros(scores.shape, F32)
    for d in range(1, N_GROUPS):
        other = pltpu.roll(gscore, GROUP_SIZE * d, axis=1)
        beats = (other > gscore) | ((other == gscore) & (grp >= d))
        rank = rank + beats.astype(F32)
    masked = jnp.where(rank < float(TOPK_GROUPS), choice, neg)
    lane_f = lane.astype(F32)
    idx_cols, w_cols = [], []
    sel = jnp.zeros(scores.shape, jnp.bool_)
    for _ in range(TOP_K):
        m = jnp.max(masked, axis=-1, keepdims=True)
        first = jnp.min(jnp.where(masked == m, lane_f, float(n_lanes)), axis=-1, keepdims=True)
        hit = lane_f == first
        idx_cols.append(first)
        w_cols.append(jnp.sum(jnp.where(hit, scores, 0.0), axis=-1, keepdims=True))
        masked = jnp.where(hit, neg, masked)
        sel = sel | hit
    return idx_cols, w_cols, sel


def _cols_to_block(cols, k_iota):
    out = jnp.zeros(k_iota.shape, cols[0].dtype)
    for k, c in enumerate(cols):
        out = jnp.where(k_iota == k, c, out)
    return out


def _mix_kernel(ya_ref, yb_ref, ga_ref, gb_ref, x_ref, mod_ref, wa_ref, wb_ref, wo_ref,
                gpm_ref, gpf_ref, wrh_ref, wrl_ref, rb_ref,
                x1_ref, h2t_ref, h2b_ref, idx_ref, wt_ref, cnt_ref):
    first = (pl.program_id(0) == 0) & (pl.program_id(1) == 0)
    merged = (ga_ref[0].astype(F32) * _dot(ya_ref[0], wa_ref[...])
              + gb_ref[0].astype(F32) * _dot(yb_ref[0], wb_ref[...]))
    o = _dot(merged.astype(BF16), wo_ref[...])
    x1 = x_ref[0] + mod_ref[0, 2:3, :] * _rmsnorm(o, gpm_ref[...])
    h2 = _rmsnorm(x1, gpf_ref[...]) * (1.0 + mod_ref[0, 4:5, :]) + mod_ref[0, 3:4, :]
    x1_ref[0] = x1
    h2b_ref[0] = h2.astype(BF16)
    tm = h2.shape[0]
    _store_row_tiles(h2t_ref, h2, tm)
    scores = jax.nn.sigmoid(_dot3(h2, wrh_ref[...], wrl_ref[...]))
    lane = lax.broadcasted_iota(I32, (tm, N_EXPERTS), 1)
    idx_cols, w_cols, sel = _route(scores, rb_ref[...], lane)
    wsum = w_cols[0]
    for c in w_cols[1:]:
        wsum = wsum + c
    k_iota = lax.broadcasted_iota(I32, (tm, TOP_K), 1)
    idx_ref[0] = _cols_to_block([c.astype(I32) for c in idx_cols], k_iota)
    wt_ref[0] = _cols_to_block([c / wsum * ROUTED_SCALE for c in w_cols], k_iota)

    @pl.when(first)
    def _():
        cnt_ref[...] = jnp.zeros_like(cnt_ref)

    cnt_ref[...] += jnp.sum(sel.astype(F32), axis=0, keepdims=True)


def _mix(ya, yb, ga, gb, x, mod, wa, wb, wo, gpm, gpf, wr_hi, wr_lo, rbias):
    bsz, s, d = x.shape
    tm = min(MIX_TM, s)
    nt = s // tm
    tok = lambda b, i: (b, i, 0)
    const = lambda b, i: (0, 0)
    return pl.pallas_call(
        _mix_kernel,
        grid=(bsz, s // tm),
        in_specs=[pl.BlockSpec((1, tm, QA_W), tok), pl.BlockSpec((1, tm, QB_W), tok),
                  pl.BlockSpec((1, tm, d), tok), pl.BlockSpec((1, tm, d), tok),
                  pl.BlockSpec((1, tm, d), tok),
                  pl.BlockSpec((1, N_MOD, d), lambda b, i: (b, 0, 0)),
                  pl.BlockSpec((QA_W, d), const), pl.BlockSpec((QB_W, d), const),
                  pl.BlockSpec((d, d), const),
                  pl.BlockSpec((1, d), const), pl.BlockSpec((1, d), const),
                  pl.BlockSpec((d, N_EXPERTS), const), pl.BlockSpec((d, N_EXPERTS), const),
                  pl.BlockSpec((1, N_EXPERTS), const)],
        out_specs=[pl.BlockSpec((1, tm, d), tok),
                   pl.BlockSpec((tm * ROW_TILE, LANES), lambda b, i: (b * nt + i, 0)),
                   pl.BlockSpec((1, tm, d), tok),
                   pl.BlockSpec((1, tm, TOP_K), tok), pl.BlockSpec((1, tm, TOP_K), tok),
                   pl.BlockSpec((1, N_EXPERTS), const)],
        out_shape=[jax.ShapeDtypeStruct((bsz, s, d), F32),
                   jax.ShapeDtypeStruct((bsz * s * ROW_TILE, LANES), F32),
                   jax.ShapeDtypeStruct((bsz, s, d), BF16),
                   jax.ShapeDtypeStruct((bsz, s, TOP_K), I32), jax.ShapeDtypeStruct((bsz, s, TOP_K), F32),
                   jax.ShapeDtypeStruct((1, N_EXPERTS), F32)],
        compiler_params=_cparams(2),
        name="mix",
    )(ya, yb, ga, gb, x, mod, wa, wb, wo, gpm, gpf, wr_hi, wr_lo, rbias)


POS_TM = 256


def _pos_kernel(idx_ref, start_ref, tri_ref, pos_ref, carry_ref):
    @pl.when(pl.program_id(0) == 0)
    def _():
        carry_ref[...] = jnp.zeros_like(carry_ref)

    idx = idx_ref[...]
    tm = idx.shape[0]
    lane = lax.broadcasted_iota(I32, (tm, N_EXPERTS), 1)
    hits = [lane == idx[:, k:k + 1] for k in range(TOP_K)]
    sel = hits[0]
    for h in hits[1:]:
        sel = sel | h
    dense = _dot(tri_ref[...], sel.astype(BF16)) + carry_ref[...] + start_ref[...]
    cols = [jnp.sum(jnp.where(h, dense, 0.0), axis=-1, keepdims=True).astype(I32) for h in hits]
    pos_ref[...] = _cols_to_block(cols, lax.broadcasted_iota(I32, (tm, TOP_K), 1))
    carry_ref[...] += jnp.sum(sel.astype(F32), axis=0, keepdims=True)


def _positions(idx, start, tri):
    t = idx.shape[0]
    tm = min(POS_TM, t)
    return pl.pallas_call(
        _pos_kernel,
        grid=(t // tm,),
        in_specs=[pl.BlockSpec((tm, TOP_K), lambda i: (i, 0)),
                  pl.BlockSpec((1, N_EXPERTS), lambda i: (0, 0)),
                  pl.BlockSpec((tm, tm), lambda i: (0, 0))],
        out_specs=pl.BlockSpec((tm, TOP_K), lambda i: (i, 0)),
        out_shape=jax.ShapeDtypeStruct((t, TOP_K), I32),
        scratch_shapes=[pltpu.VMEM((1, N_EXPERTS), F32)],
        compiler_params=_cparams(1),
        name="pos",
    )(idx, start, tri)


EXP_RING = 3

def _experts_kernel(be_ref, nu_ref, tok_ref, h_ref, wg_ref, wu_ref, wd_ref, y_ref,
                    idx_smem, xbuf, wg_b, wu_b, wd_b, sem_i, sem_x):
    i = pl.program_id(0)
    n_used = nu_ref[0]
    last = n_used - 1

    def idx_copy(j, s):
        return pltpu.make_async_copy(tok_ref.at[jnp.minimum(j, last)], idx_smem.at[s], sem_i.at[s])

    def issue_rows(si, sx, lo, hi):
        for r in range(lo, hi):
            row0 = pl.multiple_of(idx_smem[si, r] * ROW_TILE, ROW_TILE)
            pltpu.make_async_copy(h_ref.at[pl.ds(row0, ROW_TILE), :],
                                  xbuf.at[sx, pl.ds(r * ROW_TILE, ROW_TILE), :], sem_x.at[sx]).start()

    def wait_rows(sx):
        pltpu.make_async_copy(h_ref.at[pl.ds(0, MOE_BLOCK * ROW_TILE), :], xbuf.at[sx], sem_x.at[sx]).wait()

    @pl.when(i == 0)
    def _():
        idx_copy(0, 0).start()
        idx_copy(1, 1).start()
        idx_copy(0, 0).wait()
        issue_rows(0, 0, 0, MOE_BLOCK)
        idx_copy(1, 1).wait()
        issue_rows(1, 1, 0, MOE_BLOCK)
        idx_copy(2, 0).start()

    @pl.when(i < n_used)
    def _():
        @pl.when((i == 0) | (be_ref[i] != be_ref[jnp.maximum(i - 1, 0)]))
        def _():
            wg_b[...] = wg_ref[0].astype(BF16)
            wu_b[...] = wu_ref[0].astype(BF16)
            wd_b[...] = wd_ref[0].astype(BF16)

        si = i % 2
        sx = i % EXP_RING
        nx = (i + 2) % EXP_RING
        quarter = MOE_BLOCK // 4
        idx_copy(i + 2, si).wait()
        wait_rows(sx)
        xb = _load_row_tiles(xbuf.at[sx], MOE_BLOCK).astype(BF16)
        gate = _dot(xb, wg_b[...])
        issue_rows(si, nx, 0, quarter)
        up = _dot(xb, wu_b[...])
        issue_rows(si, nx, quarter, 2 * quarter)
        hid = (_silu(gate) * up).astype(BF16)
        issue_rows(si, nx, 2 * quarter, 3 * quarter)
        _store_row_tiles(y_ref, _dot(hid, wd_b[...]), MOE_BLOCK)
        issue_rows(si, nx, 3 * quarter, MOE_BLOCK)
        idx_copy(i + 3, 1 - si).start()

    @pl.when(i == last)
    def _():
        wait_rows((i + 1) % EXP_RING)
        wait_rows((i + 2) % EXP_RING)
        idx_copy(i + 3, (i + 1) % 2).wait()

    @pl.when(i >= n_used)
    def _():
        y_ref[...] = jnp.zeros_like(y_ref)


def _experts(block_e, n_used, tok_pad, h2t, wg, wu, wd):
    n_blocks = tok_pad.shape[0]
    _, d, e_dim = wg.shape
    blk_rows = MOE_BLOCK * ROW_TILE
    grid_spec = pltpu.PrefetchScalarGridSpec(
        num_scalar_prefetch=2,
        grid=(n_blocks,),
        in_specs=[pl.BlockSpec((n_blocks, MOE_BLOCK), lambda i, be, nu: (0, 0)),
                  pl.BlockSpec(memory_space=pl.ANY),
                  pl.BlockSpec((1, d, e_dim), lambda i, be, nu: (be[i], 0, 0)),
                  pl.BlockSpec((1, d, e_dim), lambda i, be, nu: (be[i], 0, 0)),
                  pl.BlockSpec((1, e_dim, d), lambda i, be, nu: (be[i], 0, 0))],
        out_specs=pl.BlockSpec((blk_rows, LANES), lambda i, be, nu: (i, 0)),
        scratch_shapes=[pltpu.SMEM((2, MOE_BLOCK), I32),
                        pltpu.VMEM((EXP_RING, blk_rows, LANES), F32),
                        pltpu.VMEM((d, e_dim), BF16), pltpu.VMEM((d, e_dim), BF16),
                        pltpu.VMEM((e_dim, d), BF16),
                        pltpu.SemaphoreType.DMA((2,)), pltpu.SemaphoreType.DMA((EXP_RING,))],
    )
    return pl.pallas_call(
        _experts_kernel,
        grid_spec=grid_spec,
        out_shape=jax.ShapeDtypeStruct((n_blocks * blk_rows, LANES), F32),
        compiler_params=_cparams(1),
        name="experts",
    )(block_e, n_used, tok_pad, h2t, wg, wu, wd)


FIN_TM = 128


def _final_kernel(pos_ref, ys_ref, wt_ref, h2_ref, x1_ref, mod_ref, wgs_ref, wus_ref, wds_ref, g_ref,
                  o_ref, idx_smem, gbuf, sem_i, sem_g):
    b, i = pl.program_id(0), pl.program_id(1)
    nt = pl.num_programs(1)
    step = b * nt + i
    last = pl.num_programs(0) * nt - 1
    cur = step % 2
    nxt = 1 - cur
    tm = o_ref.shape[1]
    rows_k = tm * ROW_TILE

    def idx_copy(j, s):
        return pltpu.make_async_copy(pos_ref.at[jnp.minimum(j, last)], idx_smem.at[s], sem_i.at[s])

    def issue_rows(s, k):
        for r in range(tm):
            row0 = pl.multiple_of(idx_smem[s, k * tm + r] * ROW_TILE, ROW_TILE)
            pltpu.make_async_copy(ys_ref.at[pl.ds(row0, ROW_TILE), :],
                                  gbuf.at[s, pl.ds((k * tm + r) * ROW_TILE, ROW_TILE), :],
                                  sem_g.at[s]).start()

    def wait_rows(s):
        pltpu.make_async_copy(ys_ref.at[pl.ds(0, TOP_K * rows_k), :], gbuf.at[s], sem_g.at[s]).wait()

    @pl.when(step == 0)
    def _():
        idx_copy(0, 0).start()
        idx_copy(0, 0).wait()
        for k in range(TOP_K):
            issue_rows(0, k)
        idx_copy(1, 1).start()

    idx_copy(step + 1, nxt).wait()
    wait_rows(cur)
    hb = h2_ref[0]
    y = _dot((_silu(_dot(hb, wgs_ref[...])) * _dot(hb, wus_ref[...])).astype(BF16), wds_ref[...])
    wt = wt_ref[0]
    for k in range(TOP_K):
        issue_rows(nxt, k)
        y = y + wt[:, k:k + 1] * _load_row_tiles(gbuf.at[cur, pl.ds(k * rows_k, rows_k), :], tm)
    idx_copy(step + 2, cur).start()
    o_ref[0] = x1_ref[0] + mod_ref[0, 5:6, :] * _rmsnorm(y, g_ref[...])

    @pl.when(step == last)
    def _():
        wait_rows(nxt)
        idx_copy(step + 2, cur).wait()


def _final(pos_tiles, ys, wts, h2, x1, mod, wgs, wus, wds, g):
    bsz, s, d = x1.shape
    tm = min(FIN_TM, s)
    nt = s // tm
    n_tiles = bsz * nt
    sd = wgs.shape[1]
    tok = lambda b, i: (b, i, 0)
    const = lambda b, i: (0, 0)
    return pl.pallas_call(
        _final_kernel,
        grid=(bsz, nt),
        in_specs=[pl.BlockSpec((n_tiles, TOP_K * tm), const),
                  pl.BlockSpec(memory_space=pl.ANY),
                  pl.BlockSpec((1, tm, TOP_K), tok),
                  pl.BlockSpec((1, tm, d), tok), pl.BlockSpec((1, tm, d), tok),
                  pl.BlockSpec((1, N_MOD, d), lambda b, i: (b, 0, 0)),
                  pl.BlockSpec((d, sd), const), pl.BlockSpec((d, sd), const), pl.BlockSpec((sd, d), const),
                  pl.BlockSpec((1, d), const)],
        out_specs=pl.BlockSpec((1, tm, d), tok),
        out_shape=jax.ShapeDtypeStruct((bsz, s, d), F32),
        scratch_shapes=[pltpu.SMEM((2, TOP_K * tm), I32),
                        pltpu.VMEM((2, TOP_K * tm * ROW_TILE, LANES), F32),
                        pltpu.SemaphoreType.DMA((2,)), pltpu.SemaphoreType.DMA((2,))],
        compiler_params=_cparams(2),
        name="final",
    )(pos_tiles, ys, wts, h2, x1, mod, wgs, wus, wds, g)


def _strict_lower(n):
    r = np.arange(n)
    return r[None, :] < r[:, None]


def kernel(x, c, w_ada, b_ada, g_pre_mix, g_post_mix, w_in, attn_sinks, rel_bias, w_branch_a, w_branch_b,
           w_out, g_pre_ffn, g_post_ffn, w_router, router_bias, w_gate_e, w_up_e, w_down_e,
           w_gate_s, w_up_s, w_down_s):
    bsz, s, d = x.shape
    t = bsz * s
    depth = w_ada.shape[0]
    bias_tab = rel_bias.astype(F32)[_t5_buckets()].transpose(2, 0, 1)
    sb_tri = jnp.asarray(_strict_lower(SB_T), BF16)
    pos_tri = jnp.asarray(_strict_lower(min(POS_TM, t)), BF16)
    for l in range(depth):
        wa_hi, wa_lo = _split(w_ada[l])
        mod = _ada(c, wa_hi, wa_lo, b_ada[l][None, :]).reshape(bsz, N_MOD, d)
        qa, ka, va, qb, kb, vb, ga, gb = _inproj(x, mod, g_pre_mix[l][None, :], w_in[l].astype(BF16))
        ya = _swa(qa, ka, va, bias_tab, attn_sinks[l])
        yb = _sb(qb, kb, vb, sb_tri)
        wr_hi, wr_lo = _split(w_router[l])
        x1, h2t, h2b, idx, wts, counts = _mix(
            ya, yb, ga, gb, x, mod, w_branch_a[l].astype(BF16), w_branch_b[l].astype(BF16),
            w_out[l].astype(BF16), g_post_mix[l][None, :], g_pre_ffn[l][None, :],
            wr_hi, wr_lo, router_bias[l][None, :])
        counts = counts[0].astype(I32)
        pad_counts = (counts + MOE_BLOCK - 1) // MOE_BLOCK * MOE_BLOCK
        pad_end = jnp.cumsum(pad_counts)
        pad_start = pad_end - pad_counts
        n_blocks = -(-(t * TOP_K) // MOE_BLOCK) + N_EXPERTS
        block_e = jnp.minimum(
            jnp.searchsorted(pad_end, jnp.arange(n_blocks, dtype=I32) * MOE_BLOCK, side='right'),
            N_EXPERTS - 1).astype(I32)
        n_used = (pad_end[-1:] // MOE_BLOCK).astype(I32)
        pos = _positions(idx.reshape(t, TOP_K), pad_start.astype(F32)[None, :], pos_tri)
        tok_flat = jnp.arange(t * TOP_K, dtype=I32) // TOP_K
        tok_pad = jnp.zeros((n_blocks * MOE_BLOCK,), I32).at[pos.reshape(-1)].set(
            tok_flat, unique_indices=True).reshape(n_blocks, MOE_BLOCK)
        ys = _experts(block_e, n_used, tok_pad, h2t, w_gate_e[l], w_up_e[l], w_down_e[l])
        tm = min(FIN_TM, s)
        pos_tiles = pos.reshape(t // tm, tm, TOP_K).transpose(0, 2, 1).reshape(t // tm, TOP_K * tm)
        x = _final(pos_tiles, ys, wts, h2b, x1, mod, w_gate_s[l].astype(BF16), w_up_s[l].astype(BF16),
                   w_down_s[l].astype(BF16), g_post_ffn[l][None, :])
    return x
```

```python
import functools

import numpy as np
import jax
import jax.numpy as jnp
from jax import lax
from jax.experimental import pallas as pl
from jax.experimental.pallas import tpu as pltpu

F32 = jnp.float32
BF16 = jnp.bfloat16
I32 = jnp.int32

D_MODEL = 1024
CHUNK = 64
HEAD_DIM = 64
SWA_HEADS = 8
SWA_KV_HEADS = 2
SWA_BLOCK = 128
WINDOW_CHUNKS = 2
SB_HEADS = 8
SB_BLOCK = 128
NUM_BUCKETS = 32
MAX_DISTANCE = 128
N_EXPERTS = 256
TOP_K = 8
N_GROUPS = 8
GROUP_SIZE = N_EXPERTS // N_GROUPS
TOPK_GROUPS = 4
EXPERT_DIM = 256
ROUTED_SCALE = 2.5
MOE_BLOCK = 128
RMS_EPS = 1e-6
N_MOD = 6
NEG_INF = -1e30

QA_W = SWA_HEADS * HEAD_DIM
KVA_W = SWA_KV_HEADS * HEAD_DIM
QB_W = SB_HEADS * HEAD_DIM
IN_WIDTH = QA_W + 2 * KVA_W + 3 * QB_W + 2 * D_MODEL
LANES = 128
HEAD_PAIRS = SB_HEADS // 2
Q_SCALE = HEAD_DIM ** -0.5
LOG2_E = 1.4426950408889634

VMEM_LIMIT = 56 * 1024 * 1024


def _cparams(n_axes, vmem=VMEM_LIMIT):
    return pltpu.CompilerParams(dimension_semantics=("arbitrary",) * n_axes, vmem_limit_bytes=vmem)


def _dot(a, b):
    return jnp.dot(a, b, preferred_element_type=F32)


def _dot_t(a, b):
    return lax.dot_general(a, b, (((1,), (1,)), ((), ())), preferred_element_type=F32)


def _split(x):
    hi = x.astype(BF16)
    lo = (x - hi.astype(F32)).astype(BF16)
    return hi, lo


def _dot3(a, b_hi, b_lo):
    a_hi, a_lo = _split(a)
    return _dot(a_hi, b_hi) + (_dot(a_hi, b_lo) + _dot(a_lo, b_hi))


def _rmsnorm(x, g):
    return x * lax.rsqrt(jnp.mean(x * x, axis=-1, keepdims=True) + RMS_EPS) * g


def _silu(x):
    return x * jax.nn.sigmoid(x)


ROW_TILE = D_MODEL // LANES


def _store_row_tiles(ref, x, n):
    for c in range(ROW_TILE):
        ref[pl.ds(c, n, stride=ROW_TILE), :] = x[:, c * LANES:(c + 1) * LANES]


def _load_row_tiles(ref, n):
    return jnp.concatenate([ref[pl.ds(c, n, stride=ROW_TILE), :] for c in range(ROW_TILE)], axis=1)


def _ada_kernel(c_ref, wh_ref, wl_ref, b_ref, o_ref):
    o_ref[...] = _dot3(_silu(c_ref[...]), wh_ref[...], wl_ref[...]) + b_ref[...]


def _ada(c, w_hi, w_lo, b):
    bsz, d = c.shape
    n = w_hi.shape[1] // d
    return pl.pallas_call(
        _ada_kernel,
        grid=(n,),
        in_specs=[pl.BlockSpec((bsz, d), lambda j: (0, 0)),
                  pl.BlockSpec((d, d), lambda j: (0, j)),
                  pl.BlockSpec((d, d), lambda j: (0, j)),
                  pl.BlockSpec((1, d), lambda j: (0, j))],
        out_specs=pl.BlockSpec((bsz, d), lambda j: (0, j)),
        out_shape=jax.ShapeDtypeStruct((bsz, n * d), F32),
        compiler_params=_cparams(1),
        name="ada",
    )(c, w_hi, w_lo, b)


INPROJ_TM = 512


def _inproj_kernel(x_ref, mod_ref, g_ref, w_ref,
                   qa_ref, ka_ref, va_ref, qb_ref, kb_ref, vb_ref, ga_ref, gb_ref):
    x = x_ref[0]
    h = _rmsnorm(x, g_ref[...]) * (1.0 + mod_ref[0, 1:2, :]) + mod_ref[0, 0:1, :]
    hb = h.astype(BF16)
    tm = x.shape[0]
    lo_half = lax.broadcasted_iota(I32, (tm, LANES), 1) < HEAD_DIM

    def proj(c0, n):
        return _dot(hb, w_ref[:, c0:c0 + n])

    def put_q(dst, base, scale):
        for c in range(0, QA_W, 256):
            dst[0, :, c:c + 256] = (proj(base + c, 256) * scale).astype(BF16)

    put_q(qa_ref, 0, Q_SCALE)
    r = proj(QA_W, 2 * KVA_W)
    for src, dst in ((r[:, :LANES], ka_ref), (r[:, LANES:], va_ref)):
        rolled = pltpu.roll(src, HEAD_DIM, axis=1)
        dst[0, :, 0 * LANES:1 * LANES] = jnp.where(lo_half, src, 0.0).astype(BF16)
        dst[0, :, 1 * LANES:2 * LANES] = jnp.where(lo_half, 0.0, rolled).astype(BF16)
        dst[0, :, 2 * LANES:3 * LANES] = jnp.where(lo_half, rolled, 0.0).astype(BF16)
        dst[0, :, 3 * LANES:4 * LANES] = jnp.where(lo_half, 0.0, src).astype(BF16)
    base_qb = QA_W + 2 * KVA_W
    put_q(qb_ref, base_qb, Q_SCALE * LOG2_E)
    for dst, base in ((kb_ref, base_qb + QB_W), (vb_ref, base_qb + 2 * QB_W)):
        for c in range(0, QB_W, 256):
            r = proj(base + c, 256)
            for t in range(2):
                pair = r[:, t * LANES:(t + 1) * LANES]
                o = 2 * (c + t * LANES)
                dst[0, :, o:o + LANES] = jnp.where(lo_half, pair, 0.0).astype(BF16)
                dst[0, :, o + LANES:o + 2 * LANES] = jnp.where(lo_half, 0.0, pair).astype(BF16)
    base_g = base_qb + 3 * QB_W
    for dst, base in ((ga_ref, base_g), (gb_ref, base_g + D_MODEL)):
        for c in range(0, D_MODEL, 256):
            dst[0, :, c:c + 256] = jax.nn.sigmoid(proj(base + c, 256)).astype(BF16)


def _inproj(x, mod, g, w_in):
    bsz, s, d = x.shape
    tm = min(INPROJ_TM, s)
    widths = (QA_W, 4 * LANES, 4 * LANES, QB_W, 2 * QB_W, 2 * QB_W, d, d)
    return pl.pallas_call(
        _inproj_kernel,
        grid=(bsz, s // tm),
        in_specs=[pl.BlockSpec((1, tm, d), lambda b, i: (b, i, 0)),
                  pl.BlockSpec((1, N_MOD, d), lambda b, i: (b, 0, 0)),
                  pl.BlockSpec((1, d), lambda b, i: (0, 0)),
                  pl.BlockSpec((d, IN_WIDTH), lambda b, i: (0, 0))],
        out_specs=[pl.BlockSpec((1, tm, w), lambda b, i: (b, i, 0)) for w in widths],
        out_shape=[jax.ShapeDtypeStruct((bsz, s, w), BF16) for w in widths],
        compiler_params=_cparams(2),
        name="inproj",
    )(x, mod, g, w_in)


def _t5_buckets():
    i = np.arange(SWA_BLOCK)[:, None]
    j = np.arange(2 * SWA_BLOCK)[None, :]
    rel = (j - SWA_BLOCK) - i
    nb = NUM_BUCKETS // 2
    bucket = (rel > 0).astype(np.int32) * nb
    n = np.abs(rel)
    max_exact = nb // 2
    large = max_exact + (np.log(np.maximum(n, 1) / max_exact)
                         / np.log(MAX_DISTANCE / max_exact) * (nb - max_exact)).astype(np.int32)
    large = np.minimum(large, nb - 1)
    return (bucket + np.where(n < max_exact, n, large)).astype(np.int32)


def _swa_kernel(sink_ref, q_ref, kp_ref, kc_ref, vp_ref, vc_ref, bias_ref, o_ref):
    n = pl.program_id(1)
    shape = (SWA_BLOCK, 2 * SWA_BLOCK)
    row_hi = lax.broadcasted_iota(I32, shape, 0) // CHUNK
    col = lax.broadcasted_iota(I32, shape, 1)
    col_chunk = col // CHUNK
    valid = (col_chunk >= row_hi) & (col_chunk <= row_hi + WINDOW_CHUNKS)
    valid = valid & ((n > 0) | (col >= SWA_BLOCK))
    for p in range(SWA_HEADS // 2):
        q = q_ref[0, :, p * LANES:(p + 1) * LANES]
        acc = jnp.zeros((SWA_BLOCK, LANES), F32)
        for r in range(2):
            h = 2 * p + r
            slot = 2 * (h // (SWA_HEADS // SWA_KV_HEADS)) + r
            sl = slice(slot * LANES, (slot + 1) * LANES)
            kcat = jnp.concatenate([kp_ref[0, :, sl], kc_ref[0, :, sl]], axis=0)
            logits = jnp.where(valid, _dot_t(q, kcat) + bias_ref[h], NEG_INF)
            sink = sink_ref[h]
            m = jnp.maximum(jnp.max(logits, axis=-1, keepdims=True), sink)
            e = jnp.exp(logits - m)
            den = jnp.sum(e, axis=-1, keepdims=True) + jnp.exp(sink - m)
            probs = (e / den).astype(BF16)
            vcat = jnp.concatenate([vp_ref[0, :, sl], vc_ref[0, :, sl]], axis=0)
            acc = acc + _dot(probs, vcat)
        o_ref[0, :, p * LANES:(p + 1) * LANES] = acc.astype(BF16)


def _swa(qa, ka, va, bias, sinks):
    bsz, s, _ = qa.shape
    nb = s // SWA_BLOCK
    cur = lambda b, n: (b, n, 0)
    prev = lambda b, n: (b, jnp.maximum(n - 1, 0), 0)
    blk = (1, SWA_BLOCK, 4 * LANES)
    return pl.pallas_call(
        _swa_kernel,
        grid=(bsz, nb),
        in_specs=[pl.BlockSpec(memory_space=pltpu.SMEM),
                  pl.BlockSpec(blk, cur),
                  pl.BlockSpec(blk, prev), pl.BlockSpec(blk, cur),
                  pl.BlockSpec(blk, prev), pl.BlockSpec(blk, cur),
                  pl.BlockSpec((SWA_HEADS, SWA_BLOCK, 2 * SWA_BLOCK), lambda b, n: (0, 0, 0))],
        out_specs=pl.BlockSpec(blk, cur),
        out_shape=jax.ShapeDtypeStruct((bsz, s, QA_W), BF16),
        compiler_params=_cparams(2),
        name="swa",
    )(sinks, qa, ka, ka, va, va, bias)


SB_T = 2 * SB_BLOCK


def _sb_kernel(q_ref, k_ref, v_ref, tri_ref, o_ref, acc_ref, c_ref):
    i = pl.program_id(1)
    tri = tri_ref[...]
    shape = (SB_T, SB_T)
    causal = lax.broadcasted_iota(I32, shape, 1) < lax.broadcasted_iota(I32, shape, 0)
    acc_ref[...] = jnp.zeros_like(acc_ref)
    c_ref[...] = jnp.zeros_like(c_ref)

    def stack(ref, r0, p):
        t = ref[0, pl.ds(r0, SB_T), 2 * p * LANES:(2 * p + 2) * LANES]
        return jnp.concatenate([t[:, :LANES], t[:, LANES:]], axis=0)

    def sweep(j, diag):
        r0 = pl.multiple_of(j * SB_T, SB_T)
        for p in range(HEAD_PAIRS):
            q = q_ref[0, :, p * LANES:(p + 1) * LANES]
            z2 = _dot_t(q, stack(k_ref, r0, p))
            ws = []
            for r in range(2):
                h = 2 * p + r
                z = z2[:, r * SB_T:(r + 1) * SB_T]
                nlk = jnp.maximum(z, 0.0) + jnp.log2(1.0 + jnp.exp2(-jnp.abs(z)))
                if diag:
                    nlk = jnp.where(causal, nlk, 0.0)
                hi, lo = _split(nlk)
                rr = _dot(jnp.concatenate([hi, lo], axis=0), tri)
                c = c_ref[h]
                lw = z - nlk - (rr[:SB_T] + rr[SB_T:])
                w = jnp.concatenate([jnp.exp2(lw[:, :LANES] - c), jnp.exp2(lw[:, LANES:] - c)], axis=1)
                if diag:
                    w = jnp.where(causal, w, 0.0)
                ws.append(w.astype(BF16))
                c_ref[h] = c + jnp.sum(nlk, axis=-1, keepdims=True)
            acc_ref[p] += _dot(jnp.concatenate(ws, axis=1), stack(v_ref, r0, p))

    sweep(i, True)

    def body(t, carry):
        sweep(i - 1 - t, False)
        return carry

    lax.fori_loop(0, i, body, 0)
    for p in range(HEAD_PAIRS):
        o_ref[0, :, p * LANES:(p + 1) * LANES] = acc_ref[p].astype(BF16)


def _sb(qb, kb, vb, tri):
    bsz, s, _ = qb.shape
    return pl.pallas_call(
        _sb_kernel,
        grid=(bsz, s // SB_T),
        in_specs=[pl.BlockSpec((1, SB_T, QB_W), lambda b, i: (b, i, 0)),
                  pl.BlockSpec((1, s, 2 * QB_W), lambda b, i: (b, 0, 0)),
                  pl.BlockSpec((1, s, 2 * QB_W), lambda b, i: (b, 0, 0)),
                  pl.BlockSpec((SB_T, SB_T), lambda b, i: (0, 0))],
        out_specs=pl.BlockSpec((1, SB_T, QB_W), lambda b, i: (b, i, 0)),
        out_shape=jax.ShapeDtypeStruct((bsz, s, QB_W), BF16),
        scratch_shapes=[pltpu.VMEM((HEAD_PAIRS, SB_T, LANES), F32),
                        pltpu.VMEM((SB_HEADS, SB_T, LANES), F32)],
        compiler_params=_cparams(2),
        name="sb",
    )(qb, kb, vb, tri)


MIX_TM = 256


def _route_t(scores, bias):
    neg = -jnp.inf
    n_e, tm = scores.shape
    choice = scores + bias
    group_rows = [choice[g * GROUP_SIZE:(g + 1) * GROUP_SIZE, :] for g in range(N_GROUPS)]
    gscore = []
    for rows in group_rows:
        m1 = jnp.max(rows, axis=0, keepdims=True)
        top = rows == m1
        n_top = jnp.sum(top.astype(F32), axis=0, keepdims=True)
        m2 = jnp.max(jnp.where(top, neg, rows), axis=0, keepdims=True)
        gscore.append(m1 + jnp.where(n_top >= 2.0, m1, m2))
    parts = []
    for g, rows in enumerate(group_rows):
        rank = jnp.zeros((1, tm), F32)
        for o in range(N_GROUPS):
            if o != g:
                beats = (gscore[o] >= gscore[g]) if o < g else (gscore[o] > gscore[g])
                rank = rank + beats.astype(F32)
        parts.append(jnp.where(rank < float(TOPK_GROUPS), rows, neg))
    masked = jnp.concatenate(parts, axis=0)
    row = lax.broadcasted_iota(I32, (n_e, tm), 0).astype(F32)
    idx_rows, w_rows = [], []
    sel = jnp.zeros((n_e, tm), jnp.bool_)
    for _ in range(TOP_K):
        m = jnp.max(masked, axis=0, keepdims=True)
        first = jnp.min(jnp.where(masked == m, row, float(n_e)), axis=0, keepdims=True)
        hit = row == first
        idx_rows.append(first)
        w_rows.append(jnp.sum(jnp.where(hit, scores, 0.0), axis=0, keepdims=True))
        masked = jnp.where(hit, neg, masked)
        sel = sel | hit
    return idx_rows, w_rows, sel


def _mix_kernel(ya_ref, yb_ref, ga_ref, gb_ref, x_ref, mod_ref, wa_ref, wb_ref, wo_ref,
                gpm_ref, gpf_ref, wrh_ref, wrl_ref, rb_ref,
                x1_ref, h2t_ref, h2b_ref, idx_ref, wt_ref, cnt_ref):
    first = (pl.program_id(0) == 0) & (pl.program_id(1) == 0)
    merged = (ga_ref[0].astype(F32) * _dot(ya_ref[0], wa_ref[...])
              + gb_ref[0].astype(F32) * _dot(yb_ref[0], wb_ref[...]))
    o = _dot(merged.astype(BF16), wo_ref[...])
    x1 = x_ref[0] + mod_ref[0, 2:3, :] * _rmsnorm(o, gpm_ref[...])
    h2 = _rmsnorm(x1, gpf_ref[...]) * (1.0 + mod_ref[0, 4:5, :]) + mod_ref[0, 3:4, :]
    x1_ref[0] = x1
    h2b_ref[0] = h2.astype(BF16)
    tm = h2.shape[0]
    _store_row_tiles(h2t_ref, h2, tm)
    h_hi, h_lo = _split(h2)
    w_hi = wrh_ref[...]
    logits = _dot_t(w_hi, h_hi) + (_dot_t(w_hi, h_lo) + _dot_t(wrl_ref[...], h_hi))
    idx_rows, w_rows, sel = _route_t(jax.nn.sigmoid(logits), rb_ref[...])
    wsum = w_rows[0]
    for r in w_rows[1:]:
        wsum = wsum + r
    for k in range(TOP_K):
        idx_ref[k:k + 1, :] = idx_rows[k].astype(I32)
        wt_ref[k:k + 1, :] = w_rows[k] / wsum * ROUTED_SCALE

    @pl.when(first)
    def _():
        cnt_ref[...] = jnp.zeros_like(cnt_ref)

    cnt_ref[...] += jnp.sum(sel.astype(F32), axis=1, keepdims=True)


def _mix(ya, yb, ga, gb, x, mod, wa, wb, wo, gpm, gpf, wrt_hi, wrt_lo, rbias):
    bsz, s, d = x.shape
    tm = min(MIX_TM, s)
    nt = s // tm
    tok = lambda b, i: (b, i, 0)
    flat = lambda b, i: (0, b * nt + i)
    const = lambda b, i: (0, 0)
    return pl.pallas_call(
        _mix_kernel,
        grid=(bsz, s // tm),
        in_specs=[pl.BlockSpec((1, tm, QA_W), tok), pl.BlockSpec((1, tm, QB_W), tok),
                  pl.BlockSpec((1, tm, d), tok), pl.BlockSpec((1, tm, d), tok),
                  pl.BlockSpec((1, tm, d), tok),
                  pl.BlockSpec((1, N_MOD, d), lambda b, i: (b, 0, 0)),
                  pl.BlockSpec((QA_W, d), const), pl.BlockSpec((QB_W, d), const),
                  pl.BlockSpec((d, d), const),
                  pl.BlockSpec((1, d), const), pl.BlockSpec((1, d), const),
                  pl.BlockSpec((N_EXPERTS, d), const), pl.BlockSpec((N_EXPERTS, d), const),
                  pl.BlockSpec((N_EXPERTS, 1), const)],
        out_specs=[pl.BlockSpec((1, tm, d), tok),
                   pl.BlockSpec((tm * ROW_TILE, LANES), lambda b, i: (b * nt + i, 0)),
                   pl.BlockSpec((1, tm, d), tok),
                   pl.BlockSpec((TOP_K, tm), flat), pl.BlockSpec((TOP_K, tm), flat),
                   pl.BlockSpec((N_EXPERTS, 1), const)],
        out_shape=[jax.ShapeDtypeStruct((bsz, s, d), F32),
                   jax.ShapeDtypeStruct((bsz * s * ROW_TILE, LANES), F32),
                   jax.ShapeDtypeStruct((bsz, s, d), BF16),
                   jax.ShapeDtypeStruct((TOP_K, bsz * s), I32), jax.ShapeDtypeStruct((TOP_K, bsz * s), F32),
                   jax.ShapeDtypeStruct((N_EXPERTS, 1), F32)],
        compiler_params=_cparams(2),
        name="mix",
    )(ya, yb, ga, gb, x, mod, wa, wb, wo, gpm, gpf, wrt_hi, wrt_lo, rbias)


POS_TM = 512


def _pos_kernel(idx_ref, start_ref, tri_ref, pos_ref, carry_ref):
    @pl.when(pl.program_id(0) == 0)
    def _():
        carry_ref[...] = jnp.zeros_like(carry_ref)

    tm = idx_ref.shape[1]
    row = lax.broadcasted_iota(I32, (N_EXPERTS, tm), 0)
    hits = [row == idx_ref[k:k + 1, :] for k in range(TOP_K)]
    sel = hits[0]
    for h in hits[1:]:
        sel = sel | h
    dense = _dot(sel.astype(BF16), tri_ref[...]) + (carry_ref[...] + start_ref[...])
    for k, h in enumerate(hits):
        pos_ref[k:k + 1, :] = jnp.sum(jnp.where(h, dense, 0.0), axis=0, keepdims=True).astype(I32)
    carry_ref[...] += jnp.sum(sel.astype(F32), axis=1, keepdims=True)


def _positions(idx, start, tri):
    t = idx.shape[1]
    tm = min(POS_TM, t)
    return pl.pallas_call(
        _pos_kernel,
        grid=(t // tm,),
        in_specs=[pl.BlockSpec((TOP_K, tm), lambda i: (0, i)),
                  pl.BlockSpec((N_EXPERTS, 1), lambda i: (0, 0)),
                  pl.BlockSpec((tm, tm), lambda i: (0, 0))],
        out_specs=pl.BlockSpec((TOP_K, tm), lambda i: (0, i)),
        out_shape=jax.ShapeDtypeStruct((TOP_K, t), I32),
        scratch_shapes=[pltpu.VMEM((N_EXPERTS, 1), F32)],
        compiler_params=_cparams(1),
        name="pos",
    )(idx, start, tri)


EXP_RING = 3

def _experts_kernel(b0_ref, nb_ref, tok_ref, h_ref, wg_ref, wu_ref, wd_ref, y_ref,
                    idx_smem, xbuf, ybuf, wg_b, wu_b, wd_b, sem_i, sem_x, sem_y):
    e = pl.program_id(0)
    n_exp = pl.num_programs(0)
    n_used = b0_ref[n_exp - 1] + nb_ref[n_exp - 1]
    last = n_used - 1
    blk_rows = MOE_BLOCK * ROW_TILE

    def idx_copy(j, s):
        return pltpu.make_async_copy(tok_ref.at[jnp.minimum(j, last)], idx_smem.at[s], sem_i.at[s])

    def issue_rows(si, sx, lo, hi):
        for r in range(lo, hi):
            row0 = pl.multiple_of(idx_smem[si, r] * ROW_TILE, ROW_TILE)
            pltpu.make_async_copy(h_ref.at[pl.ds(row0, ROW_TILE), :],
                                  xbuf.at[sx, pl.ds(r * ROW_TILE, ROW_TILE), :], sem_x.at[sx]).start()

    def wait_rows(sx):
        pltpu.make_async_copy(h_ref.at[pl.ds(0, blk_rows), :], xbuf.at[sx], sem_x.at[sx]).wait()

    def out_copy(g, s):
        dst = y_ref.at[pl.ds(pl.multiple_of(g * blk_rows, blk_rows), blk_rows), :]
        return pltpu.make_async_copy(ybuf.at[s], dst, sem_y.at[s])

    @pl.when(e == 0)
    def _():
        idx_copy(0, 0).start()
        idx_copy(1, 1).start()
        idx_copy(0, 0).wait()
        issue_rows(0, 0, 0, MOE_BLOCK)
        idx_copy(1, 1).wait()
        issue_rows(1, 1, 0, MOE_BLOCK)
        idx_copy(2, 0).start()

    wg_b[...] = wg_ref[0].astype(BF16)
    wu_b[...] = wu_ref[0].astype(BF16)
    wd_b[...] = wd_ref[0].astype(BF16)
    b0 = b0_ref[e]

    def block(j, carry):
        g = b0 + j
        si = g % 2
        sx = g % EXP_RING
        nx = (g + 2) % EXP_RING
        quarter = MOE_BLOCK // 4

        @pl.when(g >= 2)
        def _():
            out_copy(g - 2, si).wait()

        idx_copy(g + 2, si).wait()
        idx_copy(g + 3, 1 - si).start()
        wait_rows(sx)
        xb = _load_row_tiles(xbuf.at[sx], MOE_BLOCK).astype(BF16)
        gate = _dot(xb, wg_b[...])
        issue_rows(si, nx, 0, quarter)
        up = _dot(xb, wu_b[...])
        issue_rows(si, nx, quarter, 2 * quarter)
        hid = (_silu(gate) * up).astype(BF16)
        issue_rows(si, nx, 2 * quarter, 3 * quarter)
        _store_row_tiles(ybuf.at[si], _dot(hid, wd_b[...]), MOE_BLOCK)
        issue_rows(si, nx, 3 * quarter, MOE_BLOCK)
        out_copy(g, si).start()
        return carry

    lax.fori_loop(0, nb_ref[e], block, 0)

    @pl.when(e == n_exp - 1)
    def _():
        wait_rows(n_used % EXP_RING)
        wait_rows((n_used + 1) % EXP_RING)
        idx_copy(last + 3, n_used % 2).wait()
        out_copy(last, last % 2).wait()

        @pl.when(n_used >= 2)
        def _():
            out_copy(last - 1, n_used % 2).wait()

        n_blocks = y_ref.shape[0] // blk_rows
        ybuf[0] = jnp.zeros((blk_rows, LANES), F32)

        def fill(g, carry):
            out_copy(g, 0).start()
            return carry

        def fill_wait(g, carry):
            out_copy(g, 0).wait()
            return carry

        lax.fori_loop(n_used, n_blocks, fill, 0)
        lax.fori_loop(n_used, n_blocks, fill_wait, 0)


def _experts(blk_start, blk_count, tok_pad, h2t, wg, wu, wd):
    n_blocks = tok_pad.shape[0]
    n_exp, d, e_dim = wg.shape
    blk_rows = MOE_BLOCK * ROW_TILE
    grid_spec = pltpu.PrefetchScalarGridSpec(
        num_scalar_prefetch=2,
        grid=(n_exp,),
        in_specs=[pl.BlockSpec((n_blocks, MOE_BLOCK), lambda e, b0, nb: (0, 0)),
                  pl.BlockSpec(memory_space=pl.ANY),
                  pl.BlockSpec((1, d, e_dim), lambda e, b0, nb: (e, 0, 0)),
                  pl.BlockSpec((1, d, e_dim), lambda e, b0, nb: (e, 0, 0)),
                  pl.BlockSpec((1, e_dim, d), lambda e, b0, nb: (e, 0, 0))],
        out_specs=pl.BlockSpec(memory_space=pl.ANY),
        scratch_shapes=[pltpu.SMEM((2, MOE_BLOCK), I32),
                        pltpu.VMEM((EXP_RING, blk_rows, LANES), F32),
                        pltpu.VMEM((2, blk_rows, LANES), F32),
                        pltpu.VMEM((d, e_dim), BF16), pltpu.VMEM((d, e_dim), BF16),
                        pltpu.VMEM((e_dim, d), BF16),
                        pltpu.SemaphoreType.DMA((2,)), pltpu.SemaphoreType.DMA((EXP_RING,)),
                        pltpu.SemaphoreType.DMA((2,))],
    )
    return pl.pallas_call(
        _experts_kernel,
        grid_spec=grid_spec,
        out_shape=jax.ShapeDtypeStruct((n_blocks * blk_rows, LANES), F32),
        compiler_params=_cparams(1),
        name="experts",
    )(blk_start, blk_count, tok_pad, h2t, wg, wu, wd)


FIN_TM = 128


def _final_kernel(pos_ref, ys_ref, wt_ref, h2_ref, x1_ref, mod_ref, wgs_ref, wus_ref, wds_ref, g_ref,
                  o_ref, idx_smem, gbuf, sem_i, sem_g):
    b, i = pl.program_id(0), pl.program_id(1)
    nt = pl.num_programs(1)
    step = b * nt + i
    last = pl.num_programs(0) * nt - 1
    cur = step % 2
    nxt = 1 - cur
    tm = o_ref.shape[1]
    rows_k = tm * ROW_TILE

    def idx_copy(j, s):
        return pltpu.make_async_copy(pos_ref.at[jnp.minimum(j, last)], idx_smem.at[s], sem_i.at[s])

    def issue_rows(s, k):
        for r in range(tm):
            row0 = pl.multiple_of(idx_smem[s, k * tm + r] * ROW_TILE, ROW_TILE)
            pltpu.make_async_copy(ys_ref.at[pl.ds(row0, ROW_TILE), :],
                                  gbuf.at[s, pl.ds((k * tm + r) * ROW_TILE, ROW_TILE), :],
                                  sem_g.at[s]).start(priority=r % 2)

    def wait_rows(s):
        pltpu.make_async_copy(ys_ref.at[pl.ds(0, TOP_K * rows_k), :], gbuf.at[s], sem_g.at[s]).wait()

    @pl.when(step == 0)
    def _():
        idx_copy(0, 0).start()
        idx_copy(0, 0).wait()
        for k in range(TOP_K):
            issue_rows(0, k)
        idx_copy(1, 1).start()

    idx_copy(step + 1, nxt).wait()
    idx_copy(step + 2, cur).start()
    wait_rows(cur)
    hb = h2_ref[0]
    y = _dot((_silu(_dot(hb, wgs_ref[...])) * _dot(hb, wus_ref[...])).astype(BF16), wds_ref[...])
    wt = wt_ref[0]
    for k in range(TOP_K):
        issue_rows(nxt, k)
        y = y + wt[:, k:k + 1] * _load_row_tiles(gbuf.at[cur, pl.ds(k * rows_k, rows_k), :], tm)
    o_ref[0] = x1_ref[0] + mod_ref[0, 5:6, :] * _rmsnorm(y, g_ref[...])

    @pl.when(step == last)
    def _():
        wait_rows(nxt)
        idx_copy(step + 2, cur).wait()


def _final(pos_tiles, ys, wts, h2, x1, mod, wgs, wus, wds, g):
    bsz, s, d = x1.shape
    tm = min(FIN_TM, s)
    nt = s // tm
    n_tiles = bsz * nt
    sd = wgs.shape[1]
    tok = lambda b, i: (b, i, 0)
    const = lambda b, i: (0, 0)
    return pl.pallas_call(
        _final_kernel,
        grid=(bsz, nt),
        in_specs=[pl.BlockSpec((n_tiles, TOP_K * tm), const),
                  pl.BlockSpec(memory_space=pl.ANY),
                  pl.BlockSpec((1, tm, TOP_K), tok),
                  pl.BlockSpec((1, tm, d), tok), pl.BlockSpec((1, tm, d), tok),
                  pl.BlockSpec((1, N_MOD, d), lambda b, i: (b, 0, 0)),
                  pl.BlockSpec((d, sd), const), pl.BlockSpec((d, sd), const), pl.BlockSpec((sd, d), const),
                  pl.BlockSpec((1, d), const)],
        out_specs=pl.BlockSpec((1, tm, d), tok),
        out_shape=jax.ShapeDtypeStruct((bsz, s, d), F32),
        scratch_shapes=[pltpu.SMEM((2, TOP_K * tm), I32),
                        pltpu.VMEM((2, TOP_K * tm * ROW_TILE, LANES), F32),
                        pltpu.SemaphoreType.DMA((2,)), pltpu.SemaphoreType.DMA((2,))],
        compiler_params=_cparams(2),
        name="final",
    )(pos_tiles, ys, wts, h2, x1, mod, wgs, wus, wds, g)


def _strict_lower(n):
    r = np.arange(n)
    return r[None, :] < r[:, None]


def kernel(x, c, w_ada, b_ada, g_pre_mix, g_post_mix, w_in, attn_sinks, rel_bias, w_branch_a, w_branch_b,
           w_out, g_pre_ffn, g_post_ffn, w_router, router_bias, w_gate_e, w_up_e, w_down_e,
           w_gate_s, w_up_s, w_down_s):
    bsz, s, d = x.shape
    t = bsz * s
    depth = w_ada.shape[0]
    bucket_hot = (_t5_buckets()[None] == np.arange(NUM_BUCKETS)[:, None, None]).astype(np.float32)
    bias_tab = jnp.sum(bucket_hot[:, None] * rel_bias.astype(F32)[:, :, None, None], axis=0)
    sb_tri = jnp.asarray(_strict_lower(SB_T), BF16)
    pos_tri = jnp.asarray(_strict_lower(min(POS_TM, t)).T, BF16)
    for l in range(depth):
        wa_hi, wa_lo = _split(w_ada[l])
        mod = _ada(c, wa_hi, wa_lo, b_ada[l][None, :]).reshape(bsz, N_MOD, d)
        qa, ka, va, qb, kb, vb, ga, gb = _inproj(x, mod, g_pre_mix[l][None, :], w_in[l].astype(BF16))
        ya = _swa(qa, ka, va, bias_tab, attn_sinks[l])
        yb = _sb(qb, kb, vb, sb_tri)
        wrt_hi, wrt_lo = _split(w_router[l].T)
        x1, h2t, h2b, idx, wts, counts = _mix(
            ya, yb, ga, gb, x, mod, w_branch_a[l].astype(BF16), w_branch_b[l].astype(BF16),
            w_out[l].astype(BF16), g_post_mix[l][None, :], g_pre_ffn[l][None, :],
            wrt_hi, wrt_lo, router_bias[l][:, None])
        counts = counts[:, 0].astype(I32)
        pad_counts = (counts + MOE_BLOCK - 1) // MOE_BLOCK * MOE_BLOCK
        pad_end = jnp.cumsum(pad_counts)
        pad_start = pad_end - pad_counts
        n_blocks = -(-(t * TOP_K) // MOE_BLOCK) + N_EXPERTS
        pos = _positions(idx, pad_start.astype(F32)[:, None], pos_tri)
        tok_pad = jnp.zeros((n_blocks * MOE_BLOCK,), I32).at[pos.reshape(-1)].set(
            jnp.tile(jnp.arange(t, dtype=I32), TOP_K), unique_indices=True).reshape(n_blocks, MOE_BLOCK)
        ys = _experts((pad_start // MOE_BLOCK).astype(I32), (pad_counts // MOE_BLOCK).astype(I32),
                      tok_pad, h2t, w_gate_e[l], w_up_e[l], w_down_e[l])
        tm = min(FIN_TM, s)
        pos_tiles = pos.reshape(TOP_K, t // tm, tm).transpose(1, 0, 2).reshape(t // tm, TOP_K * tm)
        x = _final(pos_tiles, ys, wts.T.reshape(bsz, s, TOP_K), h2b, x1, mod, w_gate_s[l].astype(BF16), w_up_s[l].astype(BF16),
                   w_down_s[l].astype(BF16), g_post_ffn[l][None, :])
    return x
```

```python
import functools

import numpy as np
import jax
import jax.numpy as jnp
from jax import lax
from jax.experimental import pallas as pl
from jax.experimental.pallas import tpu as pltpu

F32 = jnp.float32
BF16 = jnp.bfloat16
I32 = jnp.int32

D_MODEL = 1024
CHUNK = 64
HEAD_DIM = 64
SWA_HEADS = 8
SWA_KV_HEADS = 2
SWA_BLOCK = 128
WINDOW_CHUNKS = 2
SB_HEADS = 8
SB_BLOCK = 128
NUM_BUCKETS = 32
MAX_DISTANCE = 128
N_EXPERTS = 256
TOP_K = 8
N_GROUPS = 8
GROUP_SIZE = N_EXPERTS // N_GROUPS
TOPK_GROUPS = 4
EXPERT_DIM = 256
ROUTED_SCALE = 2.5
MOE_BLOCK = 128
RMS_EPS = 1e-6
N_MOD = 6
NEG_INF = -1e30

QA_W = SWA_HEADS * HEAD_DIM
KVA_W = SWA_KV_HEADS * HEAD_DIM
QB_W = SB_HEADS * HEAD_DIM
IN_WIDTH = QA_W + 2 * KVA_W + 3 * QB_W + 2 * D_MODEL
LANES = 128
HEAD_PAIRS = SB_HEADS // 2
Q_SCALE = HEAD_DIM ** -0.5
LOG2_E = 1.4426950408889634

VMEM_LIMIT = 56 * 1024 * 1024


def _cparams(n_axes, vmem=VMEM_LIMIT):
    return pltpu.CompilerParams(dimension_semantics=("arbitrary",) * n_axes, vmem_limit_bytes=vmem)


def _dot(a, b):
    return jnp.dot(a, b, preferred_element_type=F32)


def _dot_t(a, b):
    return lax.dot_general(a, b, (((1,), (1,)), ((), ())), preferred_element_type=F32)


def _split(x):
    hi = x.astype(BF16)
    lo = (x - hi.astype(F32)).astype(BF16)
    return hi, lo


def _dot3(a, b_hi, b_lo):
    a_hi, a_lo = _split(a)
    return _dot(a_hi, b_hi) + (_dot(a_hi, b_lo) + _dot(a_lo, b_hi))


def _rmsnorm(x, g):
    return x * lax.rsqrt(jnp.mean(x * x, axis=-1, keepdims=True) + RMS_EPS) * g


def _silu(x):
    return x * jax.nn.sigmoid(x)


ROW_TILE = D_MODEL // LANES


def _store_row_tiles(ref, x, n):
    for c in range(ROW_TILE):
        ref[pl.ds(c, n, stride=ROW_TILE), :] = x[:, c * LANES:(c + 1) * LANES]


def _load_row_tiles(ref, n):
    return jnp.concatenate([ref[pl.ds(c, n, stride=ROW_TILE), :] for c in range(ROW_TILE)], axis=1)


def _ada_kernel(c_ref, wh_ref, wl_ref, b_ref, o_ref):
    o_ref[...] = _dot3(_silu(c_ref[...]), wh_ref[...], wl_ref[...]) + b_ref[...]


def _ada(c, w_hi, w_lo, b):
    bsz, d = c.shape
    n = w_hi.shape[1] // d
    return pl.pallas_call(
        _ada_kernel,
        grid=(n,),
        in_specs=[pl.BlockSpec((bsz, d), lambda j: (0, 0)),
                  pl.BlockSpec((d, d), lambda j: (0, j)),
                  pl.BlockSpec((d, d), lambda j: (0, j)),
                  pl.BlockSpec((1, d), lambda j: (0, j))],
        out_specs=pl.BlockSpec((bsz, d), lambda j: (0, j)),
        out_shape=jax.ShapeDtypeStruct((bsz, n * d), F32),
        compiler_params=_cparams(1),
        name="ada",
    )(c, w_hi, w_lo, b)


INPROJ_TM = 512


def _inproj_kernel(x_ref, mod_ref, g_ref, w_ref,
                   qa_ref, ka_ref, va_ref, qb_ref, kb_ref, vb_ref, ga_ref, gb_ref):
    x = x_ref[0]
    h = _rmsnorm(x, g_ref[...]) * (1.0 + mod_ref[0, 1:2, :]) + mod_ref[0, 0:1, :]
    hb = h.astype(BF16)
    tm = x.shape[0]
    lo_half = lax.broadcasted_iota(I32, (tm, LANES), 1) < HEAD_DIM

    def proj(c0, n):
        return _dot(hb, w_ref[:, c0:c0 + n])

    def put_q(dst, base, scale):
        for c in range(0, QA_W, 256):
            dst[0, :, c:c + 256] = (proj(base + c, 256) * scale).astype(BF16)

    put_q(qa_ref, 0, Q_SCALE)
    r = proj(QA_W, 2 * KVA_W)
    for src, dst in ((r[:, :LANES], ka_ref), (r[:, LANES:], va_ref)):
        rolled = pltpu.roll(src, HEAD_DIM, axis=1)
        dst[0, :, 0 * LANES:1 * LANES] = jnp.where(lo_half, src, 0.0).astype(BF16)
        dst[0, :, 1 * LANES:2 * LANES] = jnp.where(lo_half, 0.0, rolled).astype(BF16)
        dst[0, :, 2 * LANES:3 * LANES] = jnp.where(lo_half, rolled, 0.0).astype(BF16)
        dst[0, :, 3 * LANES:4 * LANES] = jnp.where(lo_half, 0.0, src).astype(BF16)
    base_qb = QA_W + 2 * KVA_W
    put_q(qb_ref, base_qb, Q_SCALE * LOG2_E)
    for dst, base in ((kb_ref, base_qb + QB_W), (vb_ref, base_qb + 2 * QB_W)):
        for c in range(0, QB_W, 256):
            r = proj(base + c, 256)
            for t in range(2):
                pair = r[:, t * LANES:(t + 1) * LANES]
                o = 2 * (c + t * LANES)
                dst[0, :, o:o + LANES] = jnp.where(lo_half, pair, 0.0).astype(BF16)
                dst[0, :, o + LANES:o + 2 * LANES] = jnp.where(lo_half, 0.0, pair).astype(BF16)
    base_g = base_qb + 3 * QB_W
    for dst, base in ((ga_ref, base_g), (gb_ref, base_g + D_MODEL)):
        for c in range(0, D_MODEL, 256):
            dst[0, :, c:c + 256] = jax.nn.sigmoid(proj(base + c, 256)).astype(BF16)


def _inproj(x, mod, g, w_in):
    bsz, s, d = x.shape
    tm = min(INPROJ_TM, s)
    widths = (QA_W, 4 * LANES, 4 * LANES, QB_W, 2 * QB_W, 2 * QB_W, d, d)
    return pl.pallas_call(
        _inproj_kernel,
        grid=(bsz, s // tm),
        in_specs=[pl.BlockSpec((1, tm, d), lambda b, i: (b, i, 0)),
                  pl.BlockSpec((1, N_MOD, d), lambda b, i: (b, 0, 0)),
                  pl.BlockSpec((1, d), lambda b, i: (0, 0)),
                  pl.BlockSpec((d, IN_WIDTH), lambda b, i: (0, 0))],
        out_specs=[pl.BlockSpec((1, tm, w), lambda b, i: (b, i, 0)) for w in widths],
        out_shape=[jax.ShapeDtypeStruct((bsz, s, w), BF16) for w in widths],
        compiler_params=_cparams(2),
        name="inproj",
    )(x, mod, g, w_in)


def _t5_buckets():
    i = np.arange(SWA_BLOCK)[:, None]
    j = np.arange(2 * SWA_BLOCK)[None, :]
    rel = (j - SWA_BLOCK) - i
    nb = NUM_BUCKETS // 2
    bucket = (rel > 0).astype(np.int32) * nb
    n = np.abs(rel)
    max_exact = nb // 2
    large = max_exact + (np.log(np.maximum(n, 1) / max_exact)
                         / np.log(MAX_DISTANCE / max_exact) * (nb - max_exact)).astype(np.int32)
    large = np.minimum(large, nb - 1)
    return (bucket + np.where(n < max_exact, n, large)).astype(np.int32)


def _swa_kernel(sink_ref, q_ref, kp_ref, kc_ref, vp_ref, vc_ref, bias_ref, o_ref):
    n = pl.program_id(1)
    shape = (2 * SWA_BLOCK, 2 * SWA_BLOCK)
    row = lax.broadcasted_iota(I32, shape, 0)
    row_hi = (row % SWA_BLOCK) // CHUNK
    col = lax.broadcasted_iota(I32, shape, 1)
    col_chunk = col // CHUNK
    valid = (col_chunk >= row_hi) & (col_chunk <= row_hi + WINDOW_CHUNKS)
    valid = valid & ((n > 0) | (col >= SWA_BLOCK))
    first_head = lax.broadcasted_iota(I32, (2 * SWA_BLOCK, 1), 0) < SWA_BLOCK
    group = SWA_HEADS // SWA_KV_HEADS
    for kv in range(SWA_KV_HEADS):
        q = jnp.concatenate([q_ref[0, :, (2 * kv) * LANES:(2 * kv + 1) * LANES],
                             q_ref[0, :, (2 * kv + 1) * LANES:(2 * kv + 2) * LANES]], axis=0)
        acc = jnp.zeros((2 * SWA_BLOCK, LANES), F32)
        for r in range(2):
            slot = 2 * kv + r
            sl = slice(slot * LANES, (slot + 1) * LANES)
            kcat = jnp.concatenate([kp_ref[0, :, sl], kc_ref[0, :, sl]], axis=0)
            logits = jnp.where(valid, _dot_t(q, kcat) + bias_ref[slot], NEG_INF)
            sink = jnp.where(first_head, sink_ref[group * kv + r], sink_ref[group * kv + 2 + r])
            m = jnp.maximum(jnp.max(logits, axis=-1, keepdims=True), sink)
            e = jnp.exp(logits - m)
            den = jnp.sum(e, axis=-1, keepdims=True) + jnp.exp(sink - m)
            probs = (e / den).astype(BF16)
            vcat = jnp.concatenate([vp_ref[0, :, sl], vc_ref[0, :, sl]], axis=0)
            acc = acc + _dot(probs, vcat)
        o_ref[0, :, (2 * kv) * LANES:(2 * kv + 1) * LANES] = acc[:SWA_BLOCK].astype(BF16)
        o_ref[0, :, (2 * kv + 1) * LANES:(2 * kv + 2) * LANES] = acc[SWA_BLOCK:].astype(BF16)


def _swa(qa, ka, va, bias, sinks):
    bsz, s, _ = qa.shape
    nb = s // SWA_BLOCK
    cur = lambda b, n: (b, n, 0)
    prev = lambda b, n: (b, jnp.maximum(n - 1, 0), 0)
    blk = (1, SWA_BLOCK, 4 * LANES)
    return pl.pallas_call(
        _swa_kernel,
        grid=(bsz, nb),
        in_specs=[pl.BlockSpec(memory_space=pltpu.SMEM),
                  pl.BlockSpec(blk, cur),
                  pl.BlockSpec(blk, prev), pl.BlockSpec(blk, cur),
                  pl.BlockSpec(blk, prev), pl.BlockSpec(blk, cur),
                  pl.BlockSpec((2 * SWA_KV_HEADS, 2 * SWA_BLOCK, 2 * SWA_BLOCK), lambda b, n: (0, 0, 0))],
        out_specs=pl.BlockSpec(blk, cur),
        out_shape=jax.ShapeDtypeStruct((bsz, s, QA_W), BF16),
        compiler_params=_cparams(2),
        name="swa",
    )(sinks, qa, ka, ka, va, va, bias)


SB_T = 2 * SB_BLOCK


def _sb_kernel(q_ref, k_ref, v_ref, tri_ref, o_ref, acc_ref, c_ref):
    i = pl.program_id(1)
    tri = tri_ref[...]
    shape = (SB_T, SB_T)
    causal = lax.broadcasted_iota(I32, shape, 1) < lax.broadcasted_iota(I32, shape, 0)
    acc_ref[...] = jnp.zeros_like(acc_ref)
    c_ref[...] = jnp.zeros_like(c_ref)

    def stack(ref, r0, p):
        t = ref[0, pl.ds(r0, SB_T), 2 * p * LANES:(2 * p + 2) * LANES]
        return jnp.concatenate([t[:, :LANES], t[:, LANES:]], axis=0)

    def sweep(j, diag):
        r0 = pl.multiple_of(j * SB_T, SB_T)
        for p in range(HEAD_PAIRS):
            q = q_ref[0, :, p * LANES:(p + 1) * LANES]
            z2 = _dot_t(q, stack(k_ref, r0, p))
            ws = []
            for r in range(2):
                h = 2 * p + r
                z = z2[:, r * SB_T:(r + 1) * SB_T]
                nlk = jnp.maximum(z, 0.0) + jnp.log2(1.0 + jnp.exp2(-jnp.abs(z)))
                if diag:
                    nlk = jnp.where(causal, nlk, 0.0)
                hi, lo = _split(nlk)
                rr = _dot(jnp.concatenate([hi, lo], axis=0), tri)
                c = c_ref[h]
                lw = z - nlk - (rr[:SB_T] + rr[SB_T:])
                w = jnp.concatenate([jnp.exp2(lw[:, :LANES] - c), jnp.exp2(lw[:, LANES:] - c)], axis=1)
                if diag:
                    w = jnp.where(causal, w, 0.0)
                ws.append(w.astype(BF16))
                c_ref[h] = c + jnp.sum(nlk, axis=-1, keepdims=True)
            acc_ref[p] += _dot(jnp.concatenate(ws, axis=1), stack(v_ref, r0, p))

    sweep(i, True)

    def body(t, carry):
        sweep(i - 1 - t, False)
        return carry

    lax.fori_loop(0, i, body, 0)
    for p in range(HEAD_PAIRS):
        o_ref[0, :, p * LANES:(p + 1) * LANES] = acc_ref[p].astype(BF16)


def _sb(qb, kb, vb, tri):
    bsz, s, _ = qb.shape
    return pl.pallas_call(
        _sb_kernel,
        grid=(bsz, s // SB_T),
        in_specs=[pl.BlockSpec((1, SB_T, QB_W), lambda b, i: (b, i, 0)),
                  pl.BlockSpec((1, s, 2 * QB_W), lambda b, i: (b, 0, 0)),
                  pl.BlockSpec((1, s, 2 * QB_W), lambda b, i: (b, 0, 0)),
                  pl.BlockSpec((SB_T, SB_T), lambda b, i: (0, 0))],
        out_specs=pl.BlockSpec((1, SB_T, QB_W), lambda b, i: (b, i, 0)),
        out_shape=jax.ShapeDtypeStruct((bsz, s, QB_W), BF16),
        scratch_shapes=[pltpu.VMEM((HEAD_PAIRS, SB_T, LANES), F32),
                        pltpu.VMEM((SB_HEADS, SB_T, LANES), F32)],
        compiler_params=_cparams(2),
        name="sb",
    )(qb, kb, vb, tri)


MIX_TM = 512


def _route_t(scores, bias):
    neg = -jnp.inf
    n_e, tm = scores.shape
    choice = scores + bias
    group_rows = [choice[g * GROUP_SIZE:(g + 1) * GROUP_SIZE, :] for g in range(N_GROUPS)]
    gscore = []
    for rows in group_rows:
        m1 = jnp.max(rows, axis=0, keepdims=True)
        top = rows == m1
        n_top = jnp.sum(top.astype(F32), axis=0, keepdims=True)
        m2 = jnp.max(jnp.where(top, neg, rows), axis=0, keepdims=True)
        gscore.append(m1 + jnp.where(n_top >= 2.0, m1, m2))
    parts = []
    for g, rows in enumerate(group_rows):
        rank = jnp.zeros((1, tm), F32)
        for o in range(N_GROUPS):
            if o != g:
                beats = (gscore[o] >= gscore[g]) if o < g else (gscore[o] > gscore[g])
                rank = rank + beats.astype(F32)
        parts.append(jnp.where(rank < float(TOPK_GROUPS), rows, neg))
    masked = jnp.concatenate(parts, axis=0)
    row = lax.broadcasted_iota(I32, (n_e, tm), 0).astype(F32)
    idx_rows, w_rows = [], []
    sel = jnp.zeros((n_e, tm), jnp.bool_)
    for _ in range(TOP_K):
        m = jnp.max(masked, axis=0, keepdims=True)
        first = jnp.min(jnp.where(masked == m, row, float(n_e)), axis=0, keepdims=True)
        hit = row == first
        idx_rows.append(first)
        w_rows.append(jnp.sum(jnp.where(hit, scores, 0.0), axis=0, keepdims=True))
        masked = jnp.where(hit, neg, masked)
        sel = sel | hit
    return idx_rows, w_rows, sel


def _mix_kernel(ya_ref, yb_ref, ga_ref, gb_ref, x_ref, mod_ref, wa_ref, wb_ref, wo_ref,
                gpm_ref, gpf_ref, wrh_ref, wrl_ref, rb_ref,
                x1_ref, h2t_ref, h2b_ref, idx_ref, wt_ref, cnt_ref):
    first = (pl.program_id(0) == 0) & (pl.program_id(1) == 0)
    merged = (ga_ref[0].astype(F32) * _dot(ya_ref[0], wa_ref[...])
              + gb_ref[0].astype(F32) * _dot(yb_ref[0], wb_ref[...]))
    o = _dot(merged.astype(BF16), wo_ref[...])
    x1 = x_ref[0] + mod_ref[0, 2:3, :] * _rmsnorm(o, gpm_ref[...])
    h2 = _rmsnorm(x1, gpf_ref[...]) * (1.0 + mod_ref[0, 4:5, :]) + mod_ref[0, 3:4, :]
    x1_ref[0] = x1
    h2b_ref[0] = h2.astype(BF16)
    tm = h2.shape[0]
    _store_row_tiles(h2t_ref, h2, tm)
    h_hi, h_lo = _split(h2)
    w_hi = wrh_ref[...]
    logits = _dot_t(w_hi, h_hi) + (_dot_t(w_hi, h_lo) + _dot_t(wrl_ref[...], h_hi))
    idx_rows, w_rows, sel = _route_t(jax.nn.sigmoid(logits), rb_ref[...])
    wsum = w_rows[0]
    for r in w_rows[1:]:
        wsum = wsum + r
    for k in range(TOP_K):
        idx_ref[k:k + 1, :] = idx_rows[k].astype(I32)
        wt_ref[k:k + 1, :] = w_rows[k] / wsum * ROUTED_SCALE

    @pl.when(first)
    def _():
        cnt_ref[...] = jnp.zeros_like(cnt_ref)

    cnt_ref[...] += jnp.sum(sel.astype(F32), axis=1, keepdims=True)


def _mix(ya, yb, ga, gb, x, mod, wa, wb, wo, gpm, gpf, wrt_hi, wrt_lo, rbias):
    bsz, s, d = x.shape
    tm = min(MIX_TM, s)
    nt = s // tm
    tok = lambda b, i: (b, i, 0)
    flat = lambda b, i: (0, b * nt + i)
    const = lambda b, i: (0, 0)
    return pl.pallas_call(
        _mix_kernel,
        grid=(bsz, s // tm),
        in_specs=[pl.BlockSpec((1, tm, QA_W), tok), pl.BlockSpec((1, tm, QB_W), tok),
                  pl.BlockSpec((1, tm, d), tok), pl.BlockSpec((1, tm, d), tok),
                  pl.BlockSpec((1, tm, d), tok),
                  pl.BlockSpec((1, N_MOD, d), lambda b, i: (b, 0, 0)),
                  pl.BlockSpec((QA_W, d), const), pl.BlockSpec((QB_W, d), const),
                  pl.BlockSpec((d, d), const),
                  pl.BlockSpec((1, d), const), pl.BlockSpec((1, d), const),
                  pl.BlockSpec((N_EXPERTS, d), const), pl.BlockSpec((N_EXPERTS, d), const),
                  pl.BlockSpec((N_EXPERTS, 1), const)],
        out_specs=[pl.BlockSpec((1, tm, d), tok),
                   pl.BlockSpec((tm * ROW_TILE, LANES), lambda b, i: (b * nt + i, 0)),
                   pl.BlockSpec((1, tm, d), tok),
                   pl.BlockSpec((TOP_K, tm), flat), pl.BlockSpec((TOP_K, tm), flat),
                   pl.BlockSpec((N_EXPERTS, 1), const)],
        out_shape=[jax.ShapeDtypeStruct((bsz, s, d), F32),
                   jax.ShapeDtypeStruct((bsz * s * ROW_TILE, LANES), F32),
                   jax.ShapeDtypeStruct((bsz, s, d), BF16),
                   jax.ShapeDtypeStruct((TOP_K, bsz * s), I32), jax.ShapeDtypeStruct((TOP_K, bsz * s), F32),
                   jax.ShapeDtypeStruct((N_EXPERTS, 1), F32)],
        compiler_params=_cparams(2),
        name="mix",
    )(ya, yb, ga, gb, x, mod, wa, wb, wo, gpm, gpf, wrt_hi, wrt_lo, rbias)


POS_TM = 512


DSP_UNROLL = 8


def _dispatch_kernel(idx_ref, start_ref, tri_ref, h_ref, fill_ref, pos_ref, xs_ref,
                     carry_ref, zero_ref, pos_smem, fill_smem, sem_p, sem_s):
    @pl.when(pl.program_id(0) == 0)
    def _():
        carry_ref[...] = jnp.zeros_like(carry_ref)

    tm = idx_ref.shape[1]
    n_fill = fill_ref.shape[2]
    zero_ref[...] = jnp.zeros_like(zero_ref)
    fill_to_smem = pltpu.make_async_copy(fill_ref.at[0], fill_smem, sem_p)
    fill_to_smem.start()
    fill_to_smem.wait()

    def issue_fill(r8, carry):
        for j in range(DSP_UNROLL):
            dst0 = pl.multiple_of(fill_smem[0, r8 * DSP_UNROLL + j] * ROW_TILE, ROW_TILE)
            pltpu.make_async_copy(zero_ref, xs_ref.at[pl.ds(dst0, ROW_TILE), :], sem_s).start(priority=j % 2)
        return carry

    lax.fori_loop(0, n_fill // DSP_UNROLL, issue_fill, 0)
    row = lax.broadcasted_iota(I32, (N_EXPERTS, tm), 0)
    hits = [row == idx_ref[k:k + 1, :] for k in range(TOP_K)]
    sel = hits[0]
    for h in hits[1:]:
        sel = sel | h
    dense = _dot(sel.astype(BF16), tri_ref[...]) + (carry_ref[...] + start_ref[...])
    for k, h in enumerate(hits):
        pos_ref[k:k + 1, :] = jnp.sum(jnp.where(h, dense, 0.0), axis=0, keepdims=True).astype(I32)
    carry_ref[...] += jnp.sum(sel.astype(F32), axis=1, keepdims=True)

    to_smem = pltpu.make_async_copy(pos_ref, pos_smem, sem_p)
    to_smem.start()
    to_smem.wait()

    def issue(r8, carry):
        for k in range(TOP_K):
            for j in range(DSP_UNROLL):
                r = r8 * DSP_UNROLL + j
                src0 = pl.multiple_of(r * ROW_TILE, ROW_TILE)
                dst0 = pl.multiple_of(pos_smem[k, r] * ROW_TILE, ROW_TILE)
                pltpu.make_async_copy(h_ref.at[pl.ds(src0, ROW_TILE), :], xs_ref.at[pl.ds(dst0, ROW_TILE), :],
                                      sem_s).start(priority=j % 2)
        return carry

    lax.fori_loop(0, tm // DSP_UNROLL, issue, 0)
    for n in [tm] * (TOP_K + n_fill // tm) + [n_fill % tm]:
        if n:
            pltpu.make_async_copy(h_ref.at[pl.ds(0, n * ROW_TILE), :], xs_ref.at[pl.ds(0, n * ROW_TILE), :],
                                  sem_s).wait()


def _dispatch(idx, start, tri, h2t, fill_pos):
    t = idx.shape[1]
    tm = min(POS_TM, t)
    n_steps, _, n_fill = fill_pos.shape
    n_rows = t * TOP_K + n_steps * n_fill
    return pl.pallas_call(
        _dispatch_kernel,
        grid=(n_steps,),
        in_specs=[pl.BlockSpec((TOP_K, tm), lambda i: (0, i)),
                  pl.BlockSpec((N_EXPERTS, 1), lambda i: (0, 0)),
                  pl.BlockSpec((tm, tm), lambda i: (0, 0)),
                  pl.BlockSpec((tm * ROW_TILE, LANES), lambda i: (i, 0)),
                  pl.BlockSpec((1, 1, n_fill), lambda i: (i, 0, 0))],
        out_specs=[pl.BlockSpec((TOP_K, tm), lambda i: (0, i)),
                   pl.BlockSpec(memory_space=pl.ANY)],
        out_shape=[jax.ShapeDtypeStruct((TOP_K, t), I32),
                   jax.ShapeDtypeStruct((n_rows * ROW_TILE, LANES), F32)],
        scratch_shapes=[pltpu.VMEM((N_EXPERTS, 1), F32), pltpu.VMEM((ROW_TILE, LANES), F32),
                        pltpu.SMEM((TOP_K, tm), I32), pltpu.SMEM((1, n_fill), I32),
                        pltpu.SemaphoreType.DMA, pltpu.SemaphoreType.DMA],
        compiler_params=_cparams(1),
        name="dispatch",
    )(idx, start, tri, h2t, fill_pos)


EXP_RING = 3


def _experts_kernel(b0_ref, nb_ref, xs_ref, wg_ref, wu_ref, wd_ref, y_ref,
                    xbuf, ybuf, wg_b, wu_b, wd_b, sem_x, sem_y):
    e = pl.program_id(0)
    n_exp = pl.num_programs(0)
    n_used = b0_ref[n_exp - 1] + nb_ref[n_exp - 1]
    last = n_used - 1
    blk_rows = MOE_BLOCK * ROW_TILE

    def rows_of(ref, g):
        return ref.at[pl.ds(pl.multiple_of(g * blk_rows, blk_rows), blk_rows), :]

    def in_copy(g, s):
        return pltpu.make_async_copy(rows_of(xs_ref, jnp.minimum(g, last)), xbuf.at[s], sem_x.at[s])

    def out_copy(g, s):
        return pltpu.make_async_copy(ybuf.at[s], rows_of(y_ref, g), sem_y.at[s])

    @pl.when(e == 0)
    def _():
        in_copy(0, 0).start()
        in_copy(1, 1).start()

    wg_b[...] = wg_ref[0].astype(BF16)
    wu_b[...] = wu_ref[0].astype(BF16)
    wd_b[...] = wd_ref[0].astype(BF16)
    b0 = b0_ref[e]

    def block(j, carry):
        g = b0 + j
        sy = g % 2
        sx = g % EXP_RING

        @pl.when(g >= 2)
        def _():
            out_copy(g - 2, sy).wait()

        in_copy(g + 2, (g + 2) % EXP_RING).start()
        in_copy(g, sx).wait()
        xb = _load_row_tiles(xbuf.at[sx], MOE_BLOCK).astype(BF16)
        hid = (_silu(_dot(xb, wg_b[...])) * _dot(xb, wu_b[...])).astype(BF16)
        _store_row_tiles(ybuf.at[sy], _dot(hid, wd_b[...]), MOE_BLOCK)
        out_copy(g, sy).start()
        return carry

    lax.fori_loop(0, nb_ref[e], block, 0)

    @pl.when(e == n_exp - 1)
    def _():
        in_copy(n_used, n_used % EXP_RING).wait()
        in_copy(n_used + 1, (n_used + 1) % EXP_RING).wait()
        out_copy(last, last % 2).wait()

        @pl.when(n_used >= 2)
        def _():
            out_copy(last - 1, n_used % 2).wait()

        n_blocks = y_ref.shape[0] // blk_rows
        ybuf[0] = jnp.zeros((blk_rows, LANES), F32)

        def fill(g, carry):
            out_copy(g, 0).start()
            return carry

        def fill_wait(g, carry):
            out_copy(g, 0).wait()
            return carry

        lax.fori_loop(n_used, n_blocks, fill, 0)
        lax.fori_loop(n_used, n_blocks, fill_wait, 0)


def _experts(blk_start, blk_count, xs, wg, wu, wd):
    n_exp, d, e_dim = wg.shape
    blk_rows = MOE_BLOCK * ROW_TILE
    grid_spec = pltpu.PrefetchScalarGridSpec(
        num_scalar_prefetch=2,
        grid=(n_exp,),
        in_specs=[pl.BlockSpec(memory_space=pl.ANY),
                  pl.BlockSpec((1, d, e_dim), lambda e, b0, nb: (e, 0, 0)),
                  pl.BlockSpec((1, d, e_dim), lambda e, b0, nb: (e, 0, 0)),
                  pl.BlockSpec((1, e_dim, d), lambda e, b0, nb: (e, 0, 0))],
        out_specs=pl.BlockSpec(memory_space=pl.ANY),
        scratch_shapes=[pltpu.VMEM((EXP_RING, blk_rows, LANES), F32),
                        pltpu.VMEM((2, blk_rows, LANES), F32),
                        pltpu.VMEM((d, e_dim), BF16), pltpu.VMEM((d, e_dim), BF16),
                        pltpu.VMEM((e_dim, d), BF16),
                        pltpu.SemaphoreType.DMA((EXP_RING,)), pltpu.SemaphoreType.DMA((2,))],
    )
    return pl.pallas_call(
        _experts_kernel,
        grid_spec=grid_spec,
        out_shape=jax.ShapeDtypeStruct(xs.shape, F32),
        compiler_params=_cparams(1),
        name="experts",
    )(blk_start, blk_count, xs, wg, wu, wd)


FIN_TM = 128


def _final_kernel(pos_ref, ys_ref, wt_ref, h2_ref, x1_ref, mod_ref, wgs_ref, wus_ref, wds_ref, g_ref,
                  o_ref, idx_smem, gbuf, sem_i, sem_g):
    b, i = pl.program_id(0), pl.program_id(1)
    nt = pl.num_programs(1)
    step = b * nt + i
    last = pl.num_programs(0) * nt - 1
    cur = step % 2
    nxt = 1 - cur
    tm = o_ref.shape[1]
    rows_k = tm * ROW_TILE

    def idx_copy(j, s):
        return pltpu.make_async_copy(pos_ref.at[jnp.minimum(j, last)], idx_smem.at[s], sem_i.at[s])

    def issue_rows(s, k):
        for r in range(tm):
            row0 = pl.multiple_of(idx_smem[s, k * tm + r] * ROW_TILE, ROW_TILE)
            pltpu.make_async_copy(ys_ref.at[pl.ds(row0, ROW_TILE), :],
                                  gbuf.at[s, pl.ds((k * tm + r) * ROW_TILE, ROW_TILE), :],
                                  sem_g.at[s]).start(priority=r % 2)

    def wait_rows(s):
        pltpu.make_async_copy(ys_ref.at[pl.ds(0, TOP_K * rows_k), :], gbuf.at[s], sem_g.at[s]).wait()

    @pl.when(step == 0)
    def _():
        idx_copy(0, 0).start()
        idx_copy(0, 0).wait()
        for k in range(TOP_K):
            issue_rows(0, k)
        idx_copy(1, 1).start()

    idx_copy(step + 1, nxt).wait()
    idx_copy(step + 2, cur).start()
    wait_rows(cur)
    hb = h2_ref[0]
    y = _dot((_silu(_dot(hb, wgs_ref[...])) * _dot(hb, wus_ref[...])).astype(BF16), wds_ref[...])
    wt = wt_ref[0]
    for k in range(TOP_K):
        issue_rows(nxt, k)
        y = y + wt[:, k:k + 1] * _load_row_tiles(gbuf.at[cur, pl.ds(k * rows_k, rows_k), :], tm)
    o_ref[0] = x1_ref[0] + mod_ref[0, 5:6, :] * _rmsnorm(y, g_ref[...])

    @pl.when(step == last)
    def _():
        wait_rows(nxt)
        idx_copy(step + 2, cur).wait()


def _final(pos_tiles, ys, wts, h2, x1, mod, wgs, wus, wds, g):
    bsz, s, d = x1.shape
    tm = min(FIN_TM, s)
    nt = s // tm
    n_tiles = bsz * nt
    sd = wgs.shape[1]
    tok = lambda b, i: (b, i, 0)
    const = lambda b, i: (0, 0)
    return pl.pallas_call(
        _final_kernel,
        grid=(bsz, nt),
        in_specs=[pl.BlockSpec((n_tiles, TOP_K * tm), const),
                  pl.BlockSpec(memory_space=pl.ANY),
                  pl.BlockSpec((1, tm, TOP_K), tok),
                  pl.BlockSpec((1, tm, d), tok), pl.BlockSpec((1, tm, d), tok),
                  pl.BlockSpec((1, N_MOD, d), lambda b, i: (b, 0, 0)),
                  pl.BlockSpec((d, sd), const), pl.BlockSpec((d, sd), const), pl.BlockSpec((sd, d), const),
                  pl.BlockSpec((1, d), const)],
        out_specs=pl.BlockSpec((1, tm, d), tok),
        out_shape=jax.ShapeDtypeStruct((bsz, s, d), F32),
        scratch_shapes=[pltpu.SMEM((2, TOP_K * tm), I32),
                        pltpu.VMEM((2, TOP_K * tm * ROW_TILE, LANES), F32),
                        pltpu.SemaphoreType.DMA((2,)), pltpu.SemaphoreType.DMA((2,))],
        compiler_params=_cparams(2),
        name="final",
    )(pos_tiles, ys, wts, h2, x1, mod, wgs, wus, wds, g)


def _strict_lower(n):
    r = np.arange(n)
    return r[None, :] < r[:, None]


def kernel(x, c, w_ada, b_ada, g_pre_mix, g_post_mix, w_in, attn_sinks, rel_bias, w_branch_a, w_branch_b,
           w_out, g_pre_ffn, g_post_ffn, w_router, router_bias, w_gate_e, w_up_e, w_down_e,
           w_gate_s, w_up_s, w_down_s):
    bsz, s, d = x.shape
    t = bsz * s
    depth = w_ada.shape[0]
    bucket_hot = (_t5_buckets()[None] == np.arange(NUM_BUCKETS)[:, None, None]).astype(np.float32)
    bias_tab = jnp.sum(bucket_hot[:, None] * rel_bias.astype(F32)[:, :, None, None], axis=0)
    group = SWA_HEADS // SWA_KV_HEADS
    bias_tab = jnp.stack([jnp.concatenate([bias_tab[group * kv + r], bias_tab[group * kv + 2 + r]], axis=0)
                          for kv in range(SWA_KV_HEADS) for r in range(2)])
    sb_tri = jnp.asarray(_strict_lower(SB_T), BF16)
    pos_tri = jnp.asarray(_strict_lower(min(POS_TM, t)).T, BF16)
    for l in range(depth):
        wa_hi, wa_lo = _split(w_ada[l])
        mod = _ada(c, wa_hi, wa_lo, b_ada[l][None, :]).reshape(bsz, N_MOD, d)
        qa, ka, va, qb, kb, vb, ga, gb = _inproj(x, mod, g_pre_mix[l][None, :], w_in[l].astype(BF16))
        ya = _swa(qa, ka, va, bias_tab, attn_sinks[l])
        yb = _sb(qb, kb, vb, sb_tri)
        wrt_hi, wrt_lo = _split(w_router[l].T)
        x1, h2t, h2b, idx, wts, counts = _mix(
            ya, yb, ga, gb, x, mod, w_branch_a[l].astype(BF16), w_branch_b[l].astype(BF16),
            w_out[l].astype(BF16), g_post_mix[l][None, :], g_pre_ffn[l][None, :],
            wrt_hi, wrt_lo, router_bias[l][:, None])
        counts = counts[:, 0].astype(I32)
        pad_counts = (counts + MOE_BLOCK - 1) // MOE_BLOCK * MOE_BLOCK
        pad_end = jnp.cumsum(pad_counts)
        pad_start = pad_end - pad_counts
        n_blocks = -(-(t * TOP_K) // MOE_BLOCK) + N_EXPERTS
        n_fill = n_blocks * MOE_BLOCK - t * TOP_K
        pad_n = pad_counts - counts
        pad_cum = jnp.cumsum(pad_n) - pad_n
        q = jnp.arange(n_fill, dtype=I32)
        own = (q[:, None] >= pad_cum[None, :]) & (q[:, None] < (pad_cum + pad_n)[None, :])
        inside = jnp.sum(jnp.where(own, (pad_start + counts - pad_cum)[None, :] + q[:, None], 0), axis=1)
        fill_pos = jnp.where(q < jnp.sum(pad_n), inside, pad_end[-1] + q - jnp.sum(pad_n))
        n_steps = t // min(POS_TM, t)
        pos, xs = _dispatch(idx, pad_start.astype(F32)[:, None], pos_tri, h2t,
                            fill_pos.astype(I32).reshape(n_steps, 1, n_fill // n_steps))
        ys = _experts((pad_start // MOE_BLOCK).astype(I32), (pad_counts // MOE_BLOCK).astype(I32),
                      xs, w_gate_e[l], w_up_e[l], w_down_e[l])
        tm = min(FIN_TM, s)
        pos_tiles = pos.reshape(TOP_K, t // tm, tm).transpose(1, 0, 2).reshape(t // tm, TOP_K * tm)
        x = _final(pos_tiles, ys, wts.T.reshape(bsz, s, TOP_K), h2b, x1, mod, w_gate_s[l].astype(BF16), w_up_s[l].astype(BF16),
                   w_down_s[l].astype(BF16), g_post_ffn[l][None, :])
    return x
```

```python
import functools

import numpy as np
import jax
import jax.numpy as jnp
from jax import lax
from jax.experimental import pallas as pl
from jax.experimental.pallas import tpu as pltpu

F32 = jnp.float32
BF16 = jnp.bfloat16
I32 = jnp.int32

D_MODEL = 1024
CHUNK = 64
HEAD_DIM = 64
SWA_HEADS = 8
SWA_KV_HEADS = 2
SWA_BLOCK = 128
WINDOW_CHUNKS = 2
SB_HEADS = 8
SB_BLOCK = 128
NUM_BUCKETS = 32
MAX_DISTANCE = 128
N_EXPERTS = 256
TOP_K = 8
N_GROUPS = 8
GROUP_SIZE = N_EXPERTS // N_GROUPS
TOPK_GROUPS = 4
EXPERT_DIM = 256
ROUTED_SCALE = 2.5
MOE_BLOCK = 128
RMS_EPS = 1e-6
N_MOD = 6
NEG_INF = -1e30

QA_W = SWA_HEADS * HEAD_DIM
KVA_W = SWA_KV_HEADS * HEAD_DIM
QB_W = SB_HEADS * HEAD_DIM
IN_WIDTH = QA_W + 2 * KVA_W + 3 * QB_W + 2 * D_MODEL
LANES = 128
HEAD_PAIRS = SB_HEADS // 2
Q_SCALE = HEAD_DIM ** -0.5
LOG2_E = 1.4426950408889634

VMEM_LIMIT = 56 * 1024 * 1024


def _cparams(n_axes, vmem=VMEM_LIMIT):
    return pltpu.CompilerParams(dimension_semantics=("arbitrary",) * n_axes, vmem_limit_bytes=vmem)


def _dot(a, b):
    return jnp.dot(a, b, preferred_element_type=F32)


def _dot_t(a, b):
    return lax.dot_general(a, b, (((1,), (1,)), ((), ())), preferred_element_type=F32)


def _split(x):
    hi = x.astype(BF16)
    lo = (x - hi.astype(F32)).astype(BF16)
    return hi, lo


def _dot3(a, b_hi, b_lo):
    a_hi, a_lo = _split(a)
    return _dot(a_hi, b_hi) + (_dot(a_hi, b_lo) + _dot(a_lo, b_hi))


def _rmsnorm(x, g):
    return x * lax.rsqrt(jnp.mean(x * x, axis=-1, keepdims=True) + RMS_EPS) * g


def _silu(x):
    return x * jax.nn.sigmoid(x)


U32 = jnp.uint32
ROW_TILE = D_MODEL // (2 * LANES)
HI_HALF = 0xFFFF0000


def _store_row_tiles(ref, x, n):
    half = x.shape[1] // 2
    lo = pltpu.bitcast(x[:, :half].astype(BF16).astype(F32), U32) >> 16
    hi = pltpu.bitcast(x[:, half:].astype(BF16).astype(F32), U32) & jnp.uint32(HI_HALF)
    words = hi | lo
    for c in range(ROW_TILE):
        ref[pl.ds(c, n, stride=ROW_TILE), :] = words[:, c * LANES:(c + 1) * LANES]


def _load_row_tiles(ref, n):
    words = jnp.concatenate([ref[pl.ds(c, n, stride=ROW_TILE), :] for c in range(ROW_TILE)], axis=1)
    lo = pltpu.bitcast(words << 16, F32)
    hi = pltpu.bitcast(words & jnp.uint32(HI_HALF), F32)
    return jnp.concatenate([lo, hi], axis=1)


def _ada_kernel(c_ref, wh_ref, wl_ref, b_ref, o_ref):
    o_ref[...] = _dot3(_silu(c_ref[...]), wh_ref[...], wl_ref[...]) + b_ref[...]


def _ada(c, w_hi, w_lo, b):
    bsz, d = c.shape
    n = w_hi.shape[1] // d
    return pl.pallas_call(
        _ada_kernel,
        grid=(n,),
        in_specs=[pl.BlockSpec((bsz, d), lambda j: (0, 0)),
                  pl.BlockSpec((d, d), lambda j: (0, j)),
                  pl.BlockSpec((d, d), lambda j: (0, j)),
                  pl.BlockSpec((1, d), lambda j: (0, j))],
        out_specs=pl.BlockSpec((bsz, d), lambda j: (0, j)),
        out_shape=jax.ShapeDtypeStruct((bsz, n * d), F32),
        compiler_params=_cparams(1),
        name="ada",
    )(c, w_hi, w_lo, b)


INPROJ_TM = 512


def _inproj_kernel(x_ref, mod_ref, g_ref, w_ref,
                   qa_ref, ka_ref, va_ref, qb_ref, kb_ref, vb_ref, ga_ref, gb_ref):
    x = x_ref[0]
    h = _rmsnorm(x, g_ref[...]) * (1.0 + mod_ref[0, 1:2, :]) + mod_ref[0, 0:1, :]
    hb = h.astype(BF16)
    tm = x.shape[0]
    lo_half = lax.broadcasted_iota(I32, (tm, LANES), 1) < HEAD_DIM

    def proj(c0, n):
        return _dot(hb, w_ref[:, c0:c0 + n])

    def put_q(dst, base, scale):
        for c in range(0, QA_W, 256):
            dst[0, :, c:c + 256] = (proj(base + c, 256) * scale).astype(BF16)

    put_q(qa_ref, 0, Q_SCALE)
    r = proj(QA_W, 2 * KVA_W)
    for src, dst in ((r[:, :LANES], ka_ref), (r[:, LANES:], va_ref)):
        rolled = pltpu.roll(src, HEAD_DIM, axis=1)
        dst[0, :, 0 * LANES:1 * LANES] = jnp.where(lo_half, src, 0.0).astype(BF16)
        dst[0, :, 1 * LANES:2 * LANES] = jnp.where(lo_half, 0.0, rolled).astype(BF16)
        dst[0, :, 2 * LANES:3 * LANES] = jnp.where(lo_half, rolled, 0.0).astype(BF16)
        dst[0, :, 3 * LANES:4 * LANES] = jnp.where(lo_half, 0.0, src).astype(BF16)
    base_qb = QA_W + 2 * KVA_W
    put_q(qb_ref, base_qb, Q_SCALE * LOG2_E)
    for dst, base in ((kb_ref, base_qb + QB_W), (vb_ref, base_qb + 2 * QB_W)):
        for c in range(0, QB_W, 256):
            r = proj(base + c, 256)
            for t in range(2):
                pair = r[:, t * LANES:(t + 1) * LANES]
                o = 2 * (c + t * LANES)
                dst[0, :, o:o + LANES] = jnp.where(lo_half, pair, 0.0).astype(BF16)
                dst[0, :, o + LANES:o + 2 * LANES] = jnp.where(lo_half, 0.0, pair).astype(BF16)
    base_g = base_qb + 3 * QB_W
    for dst, base in ((ga_ref, base_g), (gb_ref, base_g + D_MODEL)):
        for c in range(0, D_MODEL, 256):
            dst[0, :, c:c + 256] = jax.nn.sigmoid(proj(base + c, 256)).astype(BF16)


def _inproj(x, mod, g, w_in):
    bsz, s, d = x.shape
    tm = min(INPROJ_TM, s)
    widths = (QA_W, 4 * LANES, 4 * LANES, QB_W, 2 * QB_W, 2 * QB_W, d, d)
    return pl.pallas_call(
        _inproj_kernel,
        grid=(bsz, s // tm),
        in_specs=[pl.BlockSpec((1, tm, d), lambda b, i: (b, i, 0)),
                  pl.BlockSpec((1, N_MOD, d), lambda b, i: (b, 0, 0)),
                  pl.BlockSpec((1, d), lambda b, i: (0, 0)),
                  pl.BlockSpec((d, IN_WIDTH), lambda b, i: (0, 0))],
        out_specs=[pl.BlockSpec((1, tm, w), lambda b, i: (b, i, 0)) for w in widths],
        out_shape=[jax.ShapeDtypeStruct((bsz, s, w), BF16) for w in widths],
        compiler_params=_cparams(2),
        name="inproj",
    )(x, mod, g, w_in)


def _t5_buckets():
    i = np.arange(SWA_BLOCK)[:, None]
    j = np.arange(2 * SWA_BLOCK)[None, :]
    rel = (j - SWA_BLOCK) - i
    nb = NUM_BUCKETS // 2
    bucket = (rel > 0).astype(np.int32) * nb
    n = np.abs(rel)
    max_exact = nb // 2
    large = max_exact + (np.log(np.maximum(n, 1) / max_exact)
                         / np.log(MAX_DISTANCE / max_exact) * (nb - max_exact)).astype(np.int32)
    large = np.minimum(large, nb - 1)
    return (bucket + np.where(n < max_exact, n, large)).astype(np.int32)


def _swa_kernel(sink_ref, q_ref, kp_ref, kc_ref, vp_ref, vc_ref, bias_ref, o_ref):
    n = pl.program_id(1)
    shape = (2 * SWA_BLOCK, 2 * SWA_BLOCK)
    row = lax.broadcasted_iota(I32, shape, 0)
    row_hi = (row % SWA_BLOCK) // CHUNK
    col = lax.broadcasted_iota(I32, shape, 1)
    col_chunk = col // CHUNK
    valid = (col_chunk >= row_hi) & (col_chunk <= row_hi + WINDOW_CHUNKS)
    valid = valid & ((n > 0) | (col >= SWA_BLOCK))
    first_head = lax.broadcasted_iota(I32, (2 * SWA_BLOCK, 1), 0) < SWA_BLOCK
    group = SWA_HEADS // SWA_KV_HEADS
    for kv in range(SWA_KV_HEADS):
        q = jnp.concatenate([q_ref[0, :, (2 * kv) * LANES:(2 * kv + 1) * LANES],
                             q_ref[0, :, (2 * kv + 1) * LANES:(2 * kv + 2) * LANES]], axis=0)
        acc = jnp.zeros((2 * SWA_BLOCK, LANES), F32)
        for r in range(2):
            slot = 2 * kv + r
            sl = slice(slot * LANES, (slot + 1) * LANES)
            kcat = jnp.concatenate([kp_ref[0, :, sl], kc_ref[0, :, sl]], axis=0)
            logits = jnp.where(valid, _dot_t(q, kcat) + bias_ref[slot], NEG_INF)
            sink = jnp.where(first_head, sink_ref[group * kv + r], sink_ref[group * kv + 2 + r])
            m = jnp.maximum(jnp.max(logits, axis=-1, keepdims=True), sink)
            e = jnp.exp(logits - m)
            den = jnp.sum(e, axis=-1, keepdims=True) + jnp.exp(sink - m)
            probs = (e / den).astype(BF16)
            vcat = jnp.concatenate([vp_ref[0, :, sl], vc_ref[0, :, sl]], axis=0)
            acc = acc + _dot(probs, vcat)
        o_ref[0, :, (2 * kv) * LANES:(2 * kv + 1) * LANES] = acc[:SWA_BLOCK].astype(BF16)
        o_ref[0, :, (2 * kv + 1) * LANES:(2 * kv + 2) * LANES] = acc[SWA_BLOCK:].astype(BF16)


def _swa(qa, ka, va, bias, sinks):
    bsz, s, _ = qa.shape
    nb = s // SWA_BLOCK
    cur = lambda b, n: (b, n, 0)
    prev = lambda b, n: (b, jnp.maximum(n - 1, 0), 0)
    blk = (1, SWA_BLOCK, 4 * LANES)
    return pl.pallas_call(
        _swa_kernel,
        grid=(bsz, nb),
        in_specs=[pl.BlockSpec(memory_space=pltpu.SMEM),
                  pl.BlockSpec(blk, cur),
                  pl.BlockSpec(blk, prev), pl.BlockSpec(blk, cur),
                  pl.BlockSpec(blk, prev), pl.BlockSpec(blk, cur),
                  pl.BlockSpec((2 * SWA_KV_HEADS, 2 * SWA_BLOCK, 2 * SWA_BLOCK), lambda b, n: (0, 0, 0))],
        out_specs=pl.BlockSpec(blk, cur),
        out_shape=jax.ShapeDtypeStruct((bsz, s, QA_W), BF16),
        compiler_params=_cparams(2),
        name="swa",
    )(sinks, qa, ka, ka, va, va, bias)


SB_T = 2 * SB_BLOCK


def _sb_kernel(q_ref, k_ref, v_ref, tri_ref, o_ref, acc_ref, c_ref):
    i = pl.program_id(1)
    tri = tri_ref[...]
    shape = (SB_T, SB_T)
    causal = lax.broadcasted_iota(I32, shape, 1) < lax.broadcasted_iota(I32, shape, 0)
    acc_ref[...] = jnp.zeros_like(acc_ref)
    c_ref[...] = jnp.zeros_like(c_ref)

    def stack(ref, r0, p):
        t = ref[0, pl.ds(r0, SB_T), 2 * p * LANES:(2 * p + 2) * LANES]
        return jnp.concatenate([t[:, :LANES], t[:, LANES:]], axis=0)

    def sweep(j, diag):
        r0 = pl.multiple_of(j * SB_T, SB_T)
        for p in range(HEAD_PAIRS):
            q = q_ref[0, :, p * LANES:(p + 1) * LANES]
            z2 = _dot_t(q, stack(k_ref, r0, p))
            ws = []
            for r in range(2):
                h = 2 * p + r
                z = z2[:, r * SB_T:(r + 1) * SB_T]
                nlk = jnp.maximum(z, 0.0) + jnp.log2(1.0 + jnp.exp2(-jnp.abs(z)))
                if diag:
                    nlk = jnp.where(causal, nlk, 0.0)
                rest = _dot(nlk.astype(BF16), tri)
                c = c_ref[h]
                lw = z - nlk - rest
                w = jnp.concatenate([jnp.exp2(lw[:, :LANES] - c), jnp.exp2(lw[:, LANES:] - c)], axis=1)
                if diag:
                    w = jnp.where(causal, w, 0.0)
                ws.append(w.astype(BF16))
                c_ref[h] = c + jnp.sum(nlk, axis=-1, keepdims=True)
            acc_ref[p] += _dot(jnp.concatenate(ws, axis=1), stack(v_ref, r0, p))

    sweep(i, True)

    def body(t, carry):
        sweep(i - 1 - t, False)
        return carry

    lax.fori_loop(0, i, body, 0)
    for p in range(HEAD_PAIRS):
        o_ref[0, :, p * LANES:(p + 1) * LANES] = acc_ref[p].astype(BF16)


def _sb(qb, kb, vb, tri):
    bsz, s, _ = qb.shape
    return pl.pallas_call(
        _sb_kernel,
        grid=(bsz, s // SB_T),
        in_specs=[pl.BlockSpec((1, SB_T, QB_W), lambda b, i: (b, i, 0)),
                  pl.BlockSpec((1, s, 2 * QB_W), lambda b, i: (b, 0, 0)),
                  pl.BlockSpec((1, s, 2 * QB_W), lambda b, i: (b, 0, 0)),
                  pl.BlockSpec((SB_T, SB_T), lambda b, i: (0, 0))],
        out_specs=pl.BlockSpec((1, SB_T, QB_W), lambda b, i: (b, i, 0)),
        out_shape=jax.ShapeDtypeStruct((bsz, s, QB_W), BF16),
        scratch_shapes=[pltpu.VMEM((HEAD_PAIRS, SB_T, LANES), F32),
                        pltpu.VMEM((SB_HEADS, SB_T, LANES), F32)],
        compiler_params=_cparams(2),
        name="sb",
    )(qb, kb, vb, tri)


MIX_TM = 512


def _route_t(scores, bias):
    neg = -jnp.inf
    n_e, tm = scores.shape
    choice = scores + bias
    group_rows = [choice[g * GROUP_SIZE:(g + 1) * GROUP_SIZE, :] for g in range(N_GROUPS)]
    gscore = []
    for rows in group_rows:
        m1 = jnp.max(rows, axis=0, keepdims=True)
        top = rows == m1
        n_top = jnp.sum(top.astype(F32), axis=0, keepdims=True)
        m2 = jnp.max(jnp.where(top, neg, rows), axis=0, keepdims=True)
        gscore.append(m1 + jnp.where(n_top >= 2.0, m1, m2))
    parts = []
    for g, rows in enumerate(group_rows):
        rank = jnp.zeros((1, tm), F32)
        for o in range(N_GROUPS):
            if o != g:
                beats = (gscore[o] >= gscore[g]) if o < g else (gscore[o] > gscore[g])
                rank = rank + beats.astype(F32)
        parts.append(jnp.where(rank < float(TOPK_GROUPS), rows, neg))
    masked = jnp.concatenate(parts, axis=0)
    row = lax.broadcasted_iota(I32, (n_e, tm), 0).astype(F32)
    idx_rows, w_rows = [], []
    sel = jnp.zeros((n_e, tm), jnp.bool_)
    for _ in range(TOP_K):
        m = jnp.max(masked, axis=0, keepdims=True)
        first = jnp.min(jnp.where(masked == m, row, float(n_e)), axis=0, keepdims=True)
        hit = row == first
        idx_rows.append(first)
        w_rows.append(jnp.sum(jnp.where(hit, scores, 0.0), axis=0, keepdims=True))
        masked = jnp.where(hit, neg, masked)
        sel = sel | hit
    return idx_rows, w_rows, sel


def _mix_kernel(ya_ref, yb_ref, ga_ref, gb_ref, x_ref, mod_ref, wa_ref, wb_ref, wo_ref,
                gpm_ref, gpf_ref, wrh_ref, wrl_ref, rb_ref,
                x1_ref, h2t_ref, h2b_ref, idx_ref, wt_ref, cnt_ref):
    first = (pl.program_id(0) == 0) & (pl.program_id(1) == 0)
    merged = (ga_ref[0].astype(F32) * _dot(ya_ref[0], wa_ref[...])
              + gb_ref[0].astype(F32) * _dot(yb_ref[0], wb_ref[...]))
    o = _dot(merged.astype(BF16), wo_ref[...])
    x1 = x_ref[0] + mod_ref[0, 2:3, :] * _rmsnorm(o, gpm_ref[...])
    h2 = _rmsnorm(x1, gpf_ref[...]) * (1.0 + mod_ref[0, 4:5, :]) + mod_ref[0, 3:4, :]
    x1_ref[0] = x1
    h2b_ref[0] = h2.astype(BF16)
    tm = h2.shape[0]
    _store_row_tiles(h2t_ref, h2, tm)
    h_hi, h_lo = _split(h2)
    w_hi = wrh_ref[...]
    logits = _dot_t(w_hi, h_hi) + (_dot_t(w_hi, h_lo) + _dot_t(wrl_ref[...], h_hi))
    idx_rows, w_rows, sel = _route_t(jax.nn.sigmoid(logits), rb_ref[...])
    wsum = w_rows[0]
    for r in w_rows[1:]:
        wsum = wsum + r
    for k in range(TOP_K):
        idx_ref[k:k + 1, :] = idx_rows[k].astype(I32)
        wt_ref[k:k + 1, :] = w_rows[k] / wsum * ROUTED_SCALE

    @pl.when(first)
    def _():
        cnt_ref[...] = jnp.zeros_like(cnt_ref)

    cnt_ref[...] += jnp.sum(sel.astype(F32), axis=1, keepdims=True)


def _mix(ya, yb, ga, gb, x, mod, wa, wb, wo, gpm, gpf, wrt_hi, wrt_lo, rbias):
    bsz, s, d = x.shape
    tm = min(MIX_TM, s)
    nt = s // tm
    tok = lambda b, i: (b, i, 0)
    flat = lambda b, i: (0, b * nt + i)
    const = lambda b, i: (0, 0)
    return pl.pallas_call(
        _mix_kernel,
        grid=(bsz, s // tm),
        in_specs=[pl.BlockSpec((1, tm, QA_W), tok), pl.BlockSpec((1, tm, QB_W), tok),
                  pl.BlockSpec((1, tm, d), tok), pl.BlockSpec((1, tm, d), tok),
                  pl.BlockSpec((1, tm, d), tok),
                  pl.BlockSpec((1, N_MOD, d), lambda b, i: (b, 0, 0)),
                  pl.BlockSpec((QA_W, d), const), pl.BlockSpec((QB_W, d), const),
                  pl.BlockSpec((d, d), const),
                  pl.BlockSpec((1, d), const), pl.BlockSpec((1, d), const),
                  pl.BlockSpec((N_EXPERTS, d), const), pl.BlockSpec((N_EXPERTS, d), const),
                  pl.BlockSpec((N_EXPERTS, 1), const)],
        out_specs=[pl.BlockSpec((1, tm, d), tok),
                   pl.BlockSpec((tm * ROW_TILE, LANES), lambda b, i: (b * nt + i, 0)),
                   pl.BlockSpec((1, tm, d), tok),
                   pl.BlockSpec((TOP_K, tm), flat), pl.BlockSpec((TOP_K, tm), flat),
                   pl.BlockSpec((N_EXPERTS, 1), const)],
        out_shape=[jax.ShapeDtypeStruct((bsz, s, d), F32),
                   jax.ShapeDtypeStruct((bsz * s * ROW_TILE, LANES), U32),
                   jax.ShapeDtypeStruct((bsz, s, d), BF16),
                   jax.ShapeDtypeStruct((TOP_K, bsz * s), I32), jax.ShapeDtypeStruct((TOP_K, bsz * s), F32),
                   jax.ShapeDtypeStruct((N_EXPERTS, 1), F32)],
        compiler_params=_cparams(2),
        name="mix",
    )(ya, yb, ga, gb, x, mod, wa, wb, wo, gpm, gpf, wrt_hi, wrt_lo, rbias)


POS_TM = 512


DSP_UNROLL = 8


def _dispatch_kernel(idx_ref, start_ref, tri_ref, h_ref, fill_ref, pos_ref, xs_ref,
                     carry_ref, zero_ref, pos_smem, fill_smem, sem_p, sem_s):
    @pl.when(pl.program_id(0) == 0)
    def _():
        carry_ref[...] = jnp.zeros_like(carry_ref)

    tm = idx_ref.shape[1]
    n_fill = fill_ref.shape[2]
    zero_ref[...] = jnp.zeros_like(zero_ref)
    fill_to_smem = pltpu.make_async_copy(fill_ref.at[0], fill_smem, sem_p)
    fill_to_smem.start()
    fill_to_smem.wait()

    def issue_fill(r8, carry):
        for j in range(DSP_UNROLL):
            dst0 = pl.multiple_of(fill_smem[0, r8 * DSP_UNROLL + j] * ROW_TILE, ROW_TILE)
            pltpu.make_async_copy(zero_ref, xs_ref.at[pl.ds(dst0, ROW_TILE), :], sem_s).start(priority=j % 2)
        return carry

    lax.fori_loop(0, n_fill // DSP_UNROLL, issue_fill, 0)
    row = lax.broadcasted_iota(I32, (N_EXPERTS, tm), 0)
    hits = [row == idx_ref[k:k + 1, :] for k in range(TOP_K)]
    sel = hits[0]
    for h in hits[1:]:
        sel = sel | h
    dense = _dot(sel.astype(BF16), tri_ref[...]) + (carry_ref[...] + start_ref[...])
    for k, h in enumerate(hits):
        pos_ref[k:k + 1, :] = jnp.sum(jnp.where(h, dense, 0.0), axis=0, keepdims=True).astype(I32)
    carry_ref[...] += jnp.sum(sel.astype(F32), axis=1, keepdims=True)

    to_smem = pltpu.make_async_copy(pos_ref, pos_smem, sem_p)
    to_smem.start()
    to_smem.wait()

    def issue(r8, carry):
        for k in range(TOP_K):
            for j in range(DSP_UNROLL):
                r = r8 * DSP_UNROLL + j
                src0 = pl.multiple_of(r * ROW_TILE, ROW_TILE)
                dst0 = pl.multiple_of(pos_smem[k, r] * ROW_TILE, ROW_TILE)
                pltpu.make_async_copy(h_ref.at[pl.ds(src0, ROW_TILE), :], xs_ref.at[pl.ds(dst0, ROW_TILE), :],
                                      sem_s).start(priority=j % 2)
        return carry

    lax.fori_loop(0, tm // DSP_UNROLL, issue, 0)
    for n in [tm] * (TOP_K + n_fill // tm) + [n_fill % tm]:
        if n:
            pltpu.make_async_copy(h_ref.at[pl.ds(0, n * ROW_TILE), :], xs_ref.at[pl.ds(0, n * ROW_TILE), :],
                                  sem_s).wait()


def _dispatch(idx, start, tri, h2t, fill_pos):
    t = idx.shape[1]
    tm = min(POS_TM, t)
    n_steps, _, n_fill = fill_pos.shape
    n_rows = t * TOP_K + n_steps * n_fill
    return pl.pallas_call(
        _dispatch_kernel,
        grid=(n_steps,),
        in_specs=[pl.BlockSpec((TOP_K, tm), lambda i: (0, i)),
                  pl.BlockSpec((N_EXPERTS, 1), lambda i: (0, 0)),
                  pl.BlockSpec((tm, tm), lambda i: (0, 0)),
                  pl.BlockSpec((tm * ROW_TILE, LANES), lambda i: (i, 0)),
                  pl.BlockSpec((1, 1, n_fill), lambda i: (i, 0, 0))],
        out_specs=[pl.BlockSpec((TOP_K, tm), lambda i: (0, i)),
                   pl.BlockSpec(memory_space=pl.ANY)],
        out_shape=[jax.ShapeDtypeStruct((TOP_K, t), I32),
                   jax.ShapeDtypeStruct((n_rows * ROW_TILE, LANES), U32)],
        scratch_shapes=[pltpu.VMEM((N_EXPERTS, 1), F32), pltpu.VMEM((ROW_TILE, LANES), U32),
                        pltpu.SMEM((TOP_K, tm), I32), pltpu.SMEM((1, n_fill), I32),
                        pltpu.SemaphoreType.DMA, pltpu.SemaphoreType.DMA],
        compiler_params=_cparams(1),
        name="dispatch",
    )(idx, start, tri, h2t, fill_pos)


EXP_RING = 3


def _experts_kernel(b0_ref, nb_ref, xs_ref, wg_ref, wu_ref, wd_ref, y_ref,
                    xbuf, ybuf, wg_b, wu_b, wd_b, sem_x, sem_y):
    e = pl.program_id(0)
    n_exp = pl.num_programs(0)
    n_used = b0_ref[n_exp - 1] + nb_ref[n_exp - 1]
    last = n_used - 1
    blk_rows = MOE_BLOCK * ROW_TILE

    def rows_of(ref, g):
        return ref.at[pl.ds(pl.multiple_of(g * blk_rows, blk_rows), blk_rows), :]

    def in_copy(g, s):
        return pltpu.make_async_copy(rows_of(xs_ref, jnp.minimum(g, last)), xbuf.at[s], sem_x.at[s])

    def out_copy(g, s):
        return pltpu.make_async_copy(ybuf.at[s], rows_of(y_ref, g), sem_y.at[s])

    @pl.when(e == 0)
    def _():
        in_copy(0, 0).start()
        in_copy(1, 1).start()

    wg_b[...] = wg_ref[0].astype(BF16)
    wu_b[...] = wu_ref[0].astype(BF16)
    wd_b[...] = wd_ref[0].astype(BF16)
    b0 = b0_ref[e]

    def block(j, carry):
        g = b0 + j
        sy = g % 2
        sx = g % EXP_RING

        @pl.when(g >= 2)
        def _():
            out_copy(g - 2, sy).wait()

        in_copy(g + 2, (g + 2) % EXP_RING).start()
        in_copy(g, sx).wait()
        xb = _load_row_tiles(xbuf.at[sx], MOE_BLOCK).astype(BF16)
        hid = (_silu(_dot(xb, wg_b[...])) * _dot(xb, wu_b[...])).astype(BF16)
        _store_row_tiles(ybuf.at[sy], _dot(hid, wd_b[...]), MOE_BLOCK)
        out_copy(g, sy).start()
        return carry

    lax.fori_loop(0, nb_ref[e], block, 0)

    @pl.when(e == n_exp - 1)
    def _():
        in_copy(n_used, n_used % EXP_RING).wait()
        in_copy(n_used + 1, (n_used + 1) % EXP_RING).wait()
        out_copy(last, last % 2).wait()

        @pl.when(n_used >= 2)
        def _():
            out_copy(last - 1, n_used % 2).wait()

        n_blocks = y_ref.shape[0] // blk_rows
        ybuf[0] = jnp.zeros((blk_rows, LANES), U32)

        def fill(g, carry):
            out_copy(g, 0).start()
            return carry

        def fill_wait(g, carry):
            out_copy(g, 0).wait()
            return carry

        lax.fori_loop(n_used, n_blocks, fill, 0)
        lax.fori_loop(n_used, n_blocks, fill_wait, 0)


def _experts(blk_start, blk_count, xs, wg, wu, wd):
    n_exp, d, e_dim = wg.shape
    blk_rows = MOE_BLOCK * ROW_TILE
    grid_spec = pltpu.PrefetchScalarGridSpec(
        num_scalar_prefetch=2,
        grid=(n_exp,),
        in_specs=[pl.BlockSpec(memory_space=pl.ANY),
                  pl.BlockSpec((1, d, e_dim), lambda e, b0, nb: (e, 0, 0)),
                  pl.BlockSpec((1, d, e_dim), lambda e, b0, nb: (e, 0, 0)),
                  pl.BlockSpec((1, e_dim, d), lambda e, b0, nb: (e, 0, 0))],
        out_specs=pl.BlockSpec(memory_space=pl.ANY),
        scratch_shapes=[pltpu.VMEM((EXP_RING, blk_rows, LANES), U32),
                        pltpu.VMEM((2, blk_rows, LANES), U32),
                        pltpu.VMEM((d, e_dim), BF16), pltpu.VMEM((d, e_dim), BF16),
                        pltpu.VMEM((e_dim, d), BF16),
                        pltpu.SemaphoreType.DMA((EXP_RING,)), pltpu.SemaphoreType.DMA((2,))],
    )
    return pl.pallas_call(
        _experts_kernel,
        grid_spec=grid_spec,
        out_shape=jax.ShapeDtypeStruct(xs.shape, U32),
        compiler_params=_cparams(1),
        name="experts",
    )(blk_start, blk_count, xs, wg, wu, wd)


FIN_TM = 128


def _final_kernel(pos_ref, ys_ref, wt_ref, h2_ref, x1_ref, mod_ref, wgs_ref, wus_ref, wds_ref, g_ref,
                  o_ref, idx_smem, gbuf, sem_i, sem_g):
    b, i = pl.program_id(0), pl.program_id(1)
    nt = pl.num_programs(1)
    step = b * nt + i
    last = pl.num_programs(0) * nt - 1
    cur = step % 2
    nxt = 1 - cur
    tm = o_ref.shape[1]
    rows_k = tm * ROW_TILE

    def idx_copy(j, s):
        return pltpu.make_async_copy(pos_ref.at[jnp.minimum(j, last)], idx_smem.at[s], sem_i.at[s])

    def issue_rows(s, k):
        for r in range(tm):
            row0 = pl.multiple_of(idx_smem[s, k * tm + r] * ROW_TILE, ROW_TILE)
            pltpu.make_async_copy(ys_ref.at[pl.ds(row0, ROW_TILE), :],
                                  gbuf.at[s, pl.ds((k * tm + r) * ROW_TILE, ROW_TILE), :],
                                  sem_g.at[s]).start(priority=r % 2)

    def wait_rows(s):
        pltpu.make_async_copy(ys_ref.at[pl.ds(0, TOP_K * rows_k), :], gbuf.at[s], sem_g.at[s]).wait()

    @pl.when(step == 0)
    def _():
        idx_copy(0, 0).start()
        idx_copy(0, 0).wait()
        for k in range(TOP_K):
            issue_rows(0, k)
        idx_copy(1, 1).start()

    idx_copy(step + 1, nxt).wait()
    idx_copy(step + 2, cur).start()
    wait_rows(cur)
    hb = h2_ref[0]
    y = _dot((_silu(_dot(hb, wgs_ref[...])) * _dot(hb, wus_ref[...])).astype(BF16), wds_ref[...])
    wt = wt_ref[0]
    for k in range(TOP_K):
        issue_rows(nxt, k)
        y = y + wt[:, k:k + 1] * _load_row_tiles(gbuf.at[cur, pl.ds(k * rows_k, rows_k), :], tm)
    o_ref[0] = x1_ref[0] + mod_ref[0, 5:6, :] * _rmsnorm(y, g_ref[...])

    @pl.when(step == last)
    def _():
        wait_rows(nxt)
        idx_copy(step + 2, cur).wait()


def _final(pos_tiles, ys, wts, h2, x1, mod, wgs, wus, wds, g):
    bsz, s, d = x1.shape
    tm = min(FIN_TM, s)
    nt = s // tm
    n_tiles = bsz * nt
    sd = wgs.shape[1]
    tok = lambda b, i: (b, i, 0)
    const = lambda b, i: (0, 0)
    return pl.pallas_call(
        _final_kernel,
        grid=(bsz, nt),
        in_specs=[pl.BlockSpec((n_tiles, TOP_K * tm), const),
                  pl.BlockSpec(memory_space=pl.ANY),
                  pl.BlockSpec((1, tm, TOP_K), tok),
                  pl.BlockSpec((1, tm, d), tok), pl.BlockSpec((1, tm, d), tok),
                  pl.BlockSpec((1, N_MOD, d), lambda b, i: (b, 0, 0)),
                  pl.BlockSpec((d, sd), const), pl.BlockSpec((d, sd), const), pl.BlockSpec((sd, d), const),
                  pl.BlockSpec((1, d), const)],
        out_specs=pl.BlockSpec((1, tm, d), tok),
        out_shape=jax.ShapeDtypeStruct((bsz, s, d), F32),
        scratch_shapes=[pltpu.SMEM((2, TOP_K * tm), I32),
                        pltpu.VMEM((2, TOP_K * tm * ROW_TILE, LANES), U32),
                        pltpu.SemaphoreType.DMA((2,)), pltpu.SemaphoreType.DMA((2,))],
        compiler_params=_cparams(2),
        name="final",
    )(pos_tiles, ys, wts, h2, x1, mod, wgs, wus, wds, g)


def _strict_lower(n):
    r = np.arange(n)
    return r[None, :] < r[:, None]


def kernel(x, c, w_ada, b_ada, g_pre_mix, g_post_mix, w_in, attn_sinks, rel_bias, w_branch_a, w_branch_b,
           w_out, g_pre_ffn, g_post_ffn, w_router, router_bias, w_gate_e, w_up_e, w_down_e,
           w_gate_s, w_up_s, w_down_s):
    bsz, s, d = x.shape
    t = bsz * s
    depth = w_ada.shape[0]
    bucket_hot = (_t5_buckets()[None] == np.arange(NUM_BUCKETS)[:, None, None]).astype(np.float32)
    bias_tab = jnp.sum(bucket_hot[:, None] * rel_bias.astype(F32)[:, :, None, None], axis=0)
    group = SWA_HEADS // SWA_KV_HEADS
    bias_tab = jnp.stack([jnp.concatenate([bias_tab[group * kv + r], bias_tab[group * kv + 2 + r]], axis=0)
                          for kv in range(SWA_KV_HEADS) for r in range(2)])
    sb_tri = jnp.asarray(_strict_lower(SB_T), BF16)
    pos_tri = jnp.asarray(_strict_lower(min(POS_TM, t)).T, BF16)
    for l in range(depth):
        wa_hi, wa_lo = _split(w_ada[l])
        mod = _ada(c, wa_hi, wa_lo, b_ada[l][None, :]).reshape(bsz, N_MOD, d)
        qa, ka, va, qb, kb, vb, ga, gb = _inproj(x, mod, g_pre_mix[l][None, :], w_in[l].astype(BF16))
        ya = _swa(qa, ka, va, bias_tab, attn_sinks[l])
        yb = _sb(qb, kb, vb, sb_tri)
        wrt_hi, wrt_lo = _split(w_router[l].T)
        x1, h2t, h2b, idx, wts, counts = _mix(
            ya, yb, ga, gb, x, mod, w_branch_a[l].astype(BF16), w_branch_b[l].astype(BF16),
            w_out[l].astype(BF16), g_post_mix[l][None, :], g_pre_ffn[l][None, :],
            wrt_hi, wrt_lo, router_bias[l][:, None])
        counts = counts[:, 0].astype(I32)
        pad_counts = (counts + MOE_BLOCK - 1) // MOE_BLOCK * MOE_BLOCK
        pad_end = jnp.cumsum(pad_counts)
        pad_start = pad_end - pad_counts
        n_blocks = -(-(t * TOP_K) // MOE_BLOCK) + N_EXPERTS
        n_fill = n_blocks * MOE_BLOCK - t * TOP_K
        pad_n = pad_counts - counts
        pad_cum = jnp.cumsum(pad_n) - pad_n
        q = jnp.arange(n_fill, dtype=I32)
        own = (q[:, None] >= pad_cum[None, :]) & (q[:, None] < (pad_cum + pad_n)[None, :])
        inside = jnp.sum(jnp.where(own, (pad_start + counts - pad_cum)[None, :] + q[:, None], 0), axis=1)
        fill_pos = jnp.where(q < jnp.sum(pad_n), inside, pad_end[-1] + q - jnp.sum(pad_n))
        n_steps = t // min(POS_TM, t)
        pos, xs = _dispatch(idx, pad_start.astype(F32)[:, None], pos_tri, h2t,
                            fill_pos.astype(I32).reshape(n_steps, 1, n_fill // n_steps))
        ys = _experts((pad_start // MOE_BLOCK).astype(I32), (pad_counts // MOE_BLOCK).astype(I32),
                      xs, w_gate_e[l], w_up_e[l], w_down_e[l])
        tm = min(FIN_TM, s)
        pos_tiles = pos.reshape(TOP_K, t // tm, tm).transpose(1, 0, 2).reshape(t // tm, TOP_K * tm)
        x = _final(pos_tiles, ys, wts.T.reshape(bsz, s, TOP_K), h2b, x1, mod, w_gate_s[l].astype(BF16), w_up_s[l].astype(BF16),
                   w_down_s[l].astype(BF16), g_post_ffn[l][None, :])
    return x
```

```python
import functools

import numpy as np
import jax
import jax.numpy as jnp
from jax import lax
from jax.experimental import pallas as pl
from jax.experimental.pallas import tpu as pltpu

F32 = jnp.float32
BF16 = jnp.bfloat16
I32 = jnp.int32

D_MODEL = 1024
CHUNK = 64
HEAD_DIM = 64
SWA_HEADS = 8
SWA_KV_HEADS = 2
SWA_BLOCK = 128
WINDOW_CHUNKS = 2
SB_HEADS = 8
SB_BLOCK = 128
NUM_BUCKETS = 32
MAX_DISTANCE = 128
N_EXPERTS = 256
TOP_K = 8
N_GROUPS = 8
GROUP_SIZE = N_EXPERTS // N_GROUPS
TOPK_GROUPS = 4
EXPERT_DIM = 256
ROUTED_SCALE = 2.5
MOE_BLOCK = 128
RMS_EPS = 1e-6
N_MOD = 6
NEG_INF = -1e30

QA_W = SWA_HEADS * HEAD_DIM
KVA_W = SWA_KV_HEADS * HEAD_DIM
QB_W = SB_HEADS * HEAD_DIM
IN_WIDTH = QA_W + 2 * KVA_W + 3 * QB_W + 2 * D_MODEL
LANES = 128
HEAD_PAIRS = SB_HEADS // 2
Q_SCALE = HEAD_DIM ** -0.5
LOG2_E = 1.4426950408889634

VMEM_LIMIT = 56 * 1024 * 1024


def _cparams(n_axes, vmem=VMEM_LIMIT):
    return pltpu.CompilerParams(dimension_semantics=("arbitrary",) * n_axes, vmem_limit_bytes=vmem)


def _dot(a, b):
    return jnp.dot(a, b, preferred_element_type=F32)


def _dot_t(a, b):
    return lax.dot_general(a, b, (((1,), (1,)), ((), ())), preferred_element_type=F32)


def _split(x):
    hi = x.astype(BF16)
    lo = (x - hi.astype(F32)).astype(BF16)
    return hi, lo


def _dot3(a, b_hi, b_lo):
    a_hi, a_lo = _split(a)
    return _dot(a_hi, b_hi) + (_dot(a_hi, b_lo) + _dot(a_lo, b_hi))


def _rmsnorm(x, g):
    return x * lax.rsqrt(jnp.mean(x * x, axis=-1, keepdims=True) + RMS_EPS) * g


def _silu(x):
    return x * jax.nn.sigmoid(x)


U32 = jnp.uint32
ROW_TILE = D_MODEL // (2 * LANES)
HI_HALF = 0xFFFF0000


def _store_row_tiles(ref, x, n, exact=False):
    half = x.shape[1] // 2
    if exact:
        words = pltpu.bitcast(x[:, half:], U32) | (pltpu.bitcast(x[:, :half], U32) >> 16)
    else:
        lo = pltpu.bitcast(x[:, :half].astype(BF16).astype(F32), U32) >> 16
        hi = pltpu.bitcast(x[:, half:].astype(BF16).astype(F32), U32) & jnp.uint32(HI_HALF)
        words = hi | lo
    for c in range(ROW_TILE):
        ref[pl.ds(c, n, stride=ROW_TILE), :] = words[:, c * LANES:(c + 1) * LANES]


def _load_row_tiles(ref, n):
    words = jnp.concatenate([ref[pl.ds(c, n, stride=ROW_TILE), :] for c in range(ROW_TILE)], axis=1)
    lo = pltpu.bitcast(words << 16, F32)
    hi = pltpu.bitcast(words & jnp.uint32(HI_HALF), F32)
    return jnp.concatenate([lo, hi], axis=1)


def _ada_kernel(c_ref, wh_ref, wl_ref, b_ref, o_ref):
    o_ref[...] = _dot3(_silu(c_ref[...]), wh_ref[...], wl_ref[...]) + b_ref[...]


def _ada(c, w_hi, w_lo, b):
    bsz, d = c.shape
    n = w_hi.shape[1] // d
    return pl.pallas_call(
        _ada_kernel,
        grid=(n,),
        in_specs=[pl.BlockSpec((bsz, d), lambda j: (0, 0)),
                  pl.BlockSpec((d, d), lambda j: (0, j)),
                  pl.BlockSpec((d, d), lambda j: (0, j)),
                  pl.BlockSpec((1, d), lambda j: (0, j))],
        out_specs=pl.BlockSpec((bsz, d), lambda j: (0, j)),
        out_shape=jax.ShapeDtypeStruct((bsz, n * d), F32),
        compiler_params=_cparams(1),
        name="ada",
    )(c, w_hi, w_lo, b)


INPROJ_TM = 512


def _inproj_kernel(x_ref, mod_ref, g_ref, w_ref,
                   qa_ref, ka_ref, va_ref, qb_ref, kb_ref, vb_ref, ga_ref, gb_ref):
    x = x_ref[0]
    h = _rmsnorm(x, g_ref[...]) * (1.0 + mod_ref[0, 1:2, :]) + mod_ref[0, 0:1, :]
    hb = h.astype(BF16)
    tm = x.shape[0]
    lo_half = lax.broadcasted_iota(I32, (tm, LANES), 1) < HEAD_DIM

    def proj(c0, n):
        return _dot(hb, w_ref[:, c0:c0 + n])

    def put_q(dst, base, scale):
        for c in range(0, QA_W, 256):
            dst[0, :, c:c + 256] = (proj(base + c, 256) * scale).astype(BF16)

    put_q(qa_ref, 0, Q_SCALE)
    r = proj(QA_W, 2 * KVA_W)
    for src, dst in ((r[:, :LANES], ka_ref), (r[:, LANES:], va_ref)):
        rolled = pltpu.roll(src, HEAD_DIM, axis=1)
        dst[0, :, 0 * LANES:1 * LANES] = jnp.where(lo_half, src, 0.0).astype(BF16)
        dst[0, :, 1 * LANES:2 * LANES] = jnp.where(lo_half, 0.0, rolled).astype(BF16)
        dst[0, :, 2 * LANES:3 * LANES] = jnp.where(lo_half, rolled, 0.0).astype(BF16)
        dst[0, :, 3 * LANES:4 * LANES] = jnp.where(lo_half, 0.0, src).astype(BF16)
    base_qb = QA_W + 2 * KVA_W
    put_q(qb_ref, base_qb, Q_SCALE * LOG2_E)
    for dst, base in ((kb_ref, base_qb + QB_W), (vb_ref, base_qb + 2 * QB_W)):
        for c in range(0, QB_W, 256):
            r = proj(base + c, 256)
            for t in range(2):
                pair = r[:, t * LANES:(t + 1) * LANES]
                o = 2 * (c + t * LANES)
                dst[0, :, o:o + LANES] = jnp.where(lo_half, pair, 0.0).astype(BF16)
                dst[0, :, o + LANES:o + 2 * LANES] = jnp.where(lo_half, 0.0, pair).astype(BF16)
    base_g = base_qb + 3 * QB_W
    for dst, base in ((ga_ref, base_g), (gb_ref, base_g + D_MODEL)):
        for c in range(0, D_MODEL, 256):
            dst[0, :, c:c + 256] = jax.nn.sigmoid(proj(base + c, 256)).astype(BF16)


def _inproj(x, mod, g, w_in):
    bsz, s, d = x.shape
    tm = min(INPROJ_TM, s)
    widths = (QA_W, 4 * LANES, 4 * LANES, QB_W, 2 * QB_W, 2 * QB_W, d, d)
    return pl.pallas_call(
        _inproj_kernel,
        grid=(bsz, s // tm),
        in_specs=[pl.BlockSpec((1, tm, d), lambda b, i: (b, i, 0)),
                  pl.BlockSpec((1, N_MOD, d), lambda b, i: (b, 0, 0)),
                  pl.BlockSpec((1, d), lambda b, i: (0, 0)),
                  pl.BlockSpec((d, IN_WIDTH), lambda b, i: (0, 0))],
        out_specs=[pl.BlockSpec((1, tm, w), lambda b, i: (b, i, 0)) for w in widths],
        out_shape=[jax.ShapeDtypeStruct((bsz, s, w), BF16) for w in widths],
        compiler_params=_cparams(2),
        name="inproj",
    )(x, mod, g, w_in)


def _t5_buckets():
    i = np.arange(SWA_BLOCK)[:, None]
    j = np.arange(2 * SWA_BLOCK)[None, :]
    rel = (j - SWA_BLOCK) - i
    nb = NUM_BUCKETS // 2
    bucket = (rel > 0).astype(np.int32) * nb
    n = np.abs(rel)
    max_exact = nb // 2
    large = max_exact + (np.log(np.maximum(n, 1) / max_exact)
                         / np.log(MAX_DISTANCE / max_exact) * (nb - max_exact)).astype(np.int32)
    large = np.minimum(large, nb - 1)
    return (bucket + np.where(n < max_exact, n, large)).astype(np.int32)


def _swa_kernel(sink_ref, q_ref, kp_ref, kc_ref, vp_ref, vc_ref, bias_ref, o_ref):
    n = pl.program_id(1)
    shape = (2 * SWA_BLOCK, 2 * SWA_BLOCK)
    row = lax.broadcasted_iota(I32, shape, 0)
    row_hi = (row % SWA_BLOCK) // CHUNK
    col = lax.broadcasted_iota(I32, shape, 1)
    col_chunk = col // CHUNK
    valid = (col_chunk >= row_hi) & (col_chunk <= row_hi + WINDOW_CHUNKS)
    valid = valid & ((n > 0) | (col >= SWA_BLOCK))
    first_head = lax.broadcasted_iota(I32, (2 * SWA_BLOCK, 1), 0) < SWA_BLOCK
    group = SWA_HEADS // SWA_KV_HEADS
    for kv in range(SWA_KV_HEADS):
        q = jnp.concatenate([q_ref[0, :, (2 * kv) * LANES:(2 * kv + 1) * LANES],
                             q_ref[0, :, (2 * kv + 1) * LANES:(2 * kv + 2) * LANES]], axis=0)
        acc = jnp.zeros((2 * SWA_BLOCK, LANES), F32)
        for r in range(2):
            slot = 2 * kv + r
            sl = slice(slot * LANES, (slot + 1) * LANES)
            kcat = jnp.concatenate([kp_ref[0, :, sl], kc_ref[0, :, sl]], axis=0)
            logits = jnp.where(valid, _dot_t(q, kcat) + bias_ref[slot], NEG_INF)
            sink = jnp.where(first_head, sink_ref[group * kv + r], sink_ref[group * kv + 2 + r])
            m = jnp.maximum(jnp.max(logits, axis=-1, keepdims=True), sink)
            e = jnp.exp(logits - m)
            den = jnp.sum(e, axis=-1, keepdims=True) + jnp.exp(sink - m)
            probs = (e / den).astype(BF16)
            vcat = jnp.concatenate([vp_ref[0, :, sl], vc_ref[0, :, sl]], axis=0)
            acc = acc + _dot(probs, vcat)
        o_ref[0, :, (2 * kv) * LANES:(2 * kv + 1) * LANES] = acc[:SWA_BLOCK].astype(BF16)
        o_ref[0, :, (2 * kv + 1) * LANES:(2 * kv + 2) * LANES] = acc[SWA_BLOCK:].astype(BF16)


def _swa(qa, ka, va, bias, sinks):
    bsz, s, _ = qa.shape
    nb = s // SWA_BLOCK
    cur = lambda b, n: (b, n, 0)
    prev = lambda b, n: (b, jnp.maximum(n - 1, 0), 0)
    blk = (1, SWA_BLOCK, 4 * LANES)
    return pl.pallas_call(
        _swa_kernel,
        grid=(bsz, nb),
        in_specs=[pl.BlockSpec(memory_space=pltpu.SMEM),
                  pl.BlockSpec(blk, cur),
                  pl.BlockSpec(blk, prev), pl.BlockSpec(blk, cur),
                  pl.BlockSpec(blk, prev), pl.BlockSpec(blk, cur),
                  pl.BlockSpec((2 * SWA_KV_HEADS, 2 * SWA_BLOCK, 2 * SWA_BLOCK), lambda b, n: (0, 0, 0))],
        out_specs=pl.BlockSpec(blk, cur),
        out_shape=jax.ShapeDtypeStruct((bsz, s, QA_W), BF16),
        compiler_params=_cparams(2),
        name="swa",
    )(sinks, qa, ka, ka, va, va, bias)


SB_T = 2 * SB_BLOCK


def _sb_kernel(q_ref, k_ref, v_ref, tri_ref, o_ref, acc_ref, c_ref):
    i = pl.program_id(1)
    tri = tri_ref[...]
    shape = (SB_T, SB_T)
    causal = lax.broadcasted_iota(I32, shape, 1) < lax.broadcasted_iota(I32, shape, 0)
    acc_ref[...] = jnp.zeros_like(acc_ref)
    c_ref[...] = jnp.zeros_like(c_ref)

    def stack(ref, r0, p):
        t = ref[0, pl.ds(r0, SB_T), 2 * p * LANES:(2 * p + 2) * LANES]
        return jnp.concatenate([t[:, :LANES], t[:, LANES:]], axis=0)

    def sweep(j, diag):
        r0 = pl.multiple_of(j * SB_T, SB_T)
        for p in range(HEAD_PAIRS):
            q = q_ref[0, :, p * LANES:(p + 1) * LANES]
            z2 = _dot_t(q, stack(k_ref, r0, p))
            ws = []
            for r in range(2):
                h = 2 * p + r
                z = z2[:, r * SB_T:(r + 1) * SB_T]
                nlk = jnp.maximum(z, 0.0) + jnp.log2(1.0 + jnp.exp2(-jnp.abs(z)))
                if diag:
                    nlk = jnp.where(causal, nlk, 0.0)
                rest = _dot(nlk.astype(BF16), tri)
                c = c_ref[h]
                lw = z - nlk - rest
                w = jnp.concatenate([jnp.exp2(lw[:, :LANES] - c), jnp.exp2(lw[:, LANES:] - c)], axis=1)
                if diag:
                    w = jnp.where(causal, w, 0.0)
                ws.append(w.astype(BF16))
                c_ref[h] = c + jnp.sum(nlk, axis=-1, keepdims=True)
            acc_ref[p] += _dot(jnp.concatenate(ws, axis=1), stack(v_ref, r0, p))

    sweep(i, True)

    def body(t, carry):
        sweep(i - 1 - t, False)
        return carry

    lax.fori_loop(0, i, body, 0)
    for p in range(HEAD_PAIRS):
        o_ref[0, :, p * LANES:(p + 1) * LANES] = acc_ref[p].astype(BF16)


def _sb(qb, kb, vb, tri):
    bsz, s, _ = qb.shape
    return pl.pallas_call(
        _sb_kernel,
        grid=(bsz, s // SB_T),
        in_specs=[pl.BlockSpec((1, SB_T, QB_W), lambda b, i: (b, i, 0)),
                  pl.BlockSpec((1, s, 2 * QB_W), lambda b, i: (b, 0, 0)),
                  pl.BlockSpec((1, s, 2 * QB_W), lambda b, i: (b, 0, 0)),
                  pl.BlockSpec((SB_T, SB_T), lambda b, i: (0, 0))],
        out_specs=pl.BlockSpec((1, SB_T, QB_W), lambda b, i: (b, i, 0)),
        out_shape=jax.ShapeDtypeStruct((bsz, s, QB_W), BF16),
        scratch_shapes=[pltpu.VMEM((HEAD_PAIRS, SB_T, LANES), F32),
                        pltpu.VMEM((SB_HEADS, SB_T, LANES), F32)],
        compiler_params=_cparams(2),
        name="sb",
    )(qb, kb, vb, tri)


MIX_TM = 512


def _route_t(scores, bias):
    neg = -jnp.inf
    n_e, tm = scores.shape
    choice = scores + bias
    group_rows = [choice[g * GROUP_SIZE:(g + 1) * GROUP_SIZE, :] for g in range(N_GROUPS)]
    gscore = []
    for rows in group_rows:
        m1 = jnp.max(rows, axis=0, keepdims=True)
        top = rows == m1
        n_top = jnp.sum(top.astype(F32), axis=0, keepdims=True)
        m2 = jnp.max(jnp.where(top, neg, rows), axis=0, keepdims=True)
        gscore.append(m1 + jnp.where(n_top >= 2.0, m1, m2))
    parts = []
    for g, rows in enumerate(group_rows):
        rank = jnp.zeros((1, tm), F32)
        for o in range(N_GROUPS):
            if o != g:
                beats = (gscore[o] >= gscore[g]) if o < g else (gscore[o] > gscore[g])
                rank = rank + beats.astype(F32)
        parts.append(jnp.where(rank < float(TOPK_GROUPS), rows, neg))
    masked = jnp.concatenate(parts, axis=0)
    row = lax.broadcasted_iota(I32, (n_e, tm), 0).astype(F32)
    idx_rows, w_rows = [], []
    sel = jnp.zeros((n_e, tm), jnp.bool_)
    for _ in range(TOP_K):
        m = jnp.max(masked, axis=0, keepdims=True)
        first = jnp.min(jnp.where(masked == m, row, float(n_e)), axis=0, keepdims=True)
        hit = row == first
        idx_rows.append(first)
        w_rows.append(jnp.sum(jnp.where(hit, scores, 0.0), axis=0, keepdims=True))
        masked = jnp.where(hit, neg, masked)
        sel = sel | hit
    return idx_rows, w_rows, sel


def _mix_kernel(ya_ref, yb_ref, ga_ref, gb_ref, x_ref, mod_ref, wa_ref, wb_ref, wo_ref,
                gpm_ref, gpf_ref, wrh_ref, wrl_ref, rb_ref,
                x1_ref, h2b_ref, idx_ref, wt_ref, cnt_ref):
    first = (pl.program_id(0) == 0) & (pl.program_id(1) == 0)
    merged = (ga_ref[0].astype(F32) * _dot(ya_ref[0], wa_ref[...])
              + gb_ref[0].astype(F32) * _dot(yb_ref[0], wb_ref[...]))
    o = _dot(merged.astype(BF16), wo_ref[...])
    x1 = x_ref[0] + mod_ref[0, 2:3, :] * _rmsnorm(o, gpm_ref[...])
    h2 = _rmsnorm(x1, gpf_ref[...]) * (1.0 + mod_ref[0, 4:5, :]) + mod_ref[0, 3:4, :]
    x1_ref[0] = x1
    h2b_ref[0] = h2.astype(BF16)
    h_hi, h_lo = _split(h2)
    w_hi = wrh_ref[...]
    logits = _dot_t(w_hi, h_hi) + (_dot_t(w_hi, h_lo) + _dot_t(wrl_ref[...], h_hi))
    idx_rows, w_rows, sel = _route_t(jax.nn.sigmoid(logits), rb_ref[...])
    wsum = w_rows[0]
    for r in w_rows[1:]:
        wsum = wsum + r
    for k in range(TOP_K):
        idx_ref[k:k + 1, :] = idx_rows[k].astype(I32)
        wt_ref[k:k + 1, :] = w_rows[k] / wsum * ROUTED_SCALE

    @pl.when(first)
    def _():
        cnt_ref[...] = jnp.zeros_like(cnt_ref)

    cnt_ref[...] += jnp.sum(sel.astype(F32), axis=1, keepdims=True)


def _mix(ya, yb, ga, gb, x, mod, wa, wb, wo, gpm, gpf, wrt_hi, wrt_lo, rbias):
    bsz, s, d = x.shape
    tm = min(MIX_TM, s)
    nt = s // tm
    tok = lambda b, i: (b, i, 0)
    flat = lambda b, i: (0, b * nt + i)
    const = lambda b, i: (0, 0)
    return pl.pallas_call(
        _mix_kernel,
        grid=(bsz, s // tm),
        in_specs=[pl.BlockSpec((1, tm, QA_W), tok), pl.BlockSpec((1, tm, QB_W), tok),
                  pl.BlockSpec((1, tm, d), tok), pl.BlockSpec((1, tm, d), tok),
                  pl.BlockSpec((1, tm, d), tok),
                  pl.BlockSpec((1, N_MOD, d), lambda b, i: (b, 0, 0)),
                  pl.BlockSpec((QA_W, d), const), pl.BlockSpec((QB_W, d), const),
                  pl.BlockSpec((d, d), const),
                  pl.BlockSpec((1, d), const), pl.BlockSpec((1, d), const),
                  pl.BlockSpec((N_EXPERTS, d), const), pl.BlockSpec((N_EXPERTS, d), const),
                  pl.BlockSpec((N_EXPERTS, 1), const)],
        out_specs=[pl.BlockSpec((1, tm, d), tok),
                   pl.BlockSpec((1, tm, d), tok),
                   pl.BlockSpec((TOP_K, tm), flat), pl.BlockSpec((TOP_K, tm), flat),
                   pl.BlockSpec((N_EXPERTS, 1), const)],
        out_shape=[jax.ShapeDtypeStruct((bsz, s, d), F32),
                   jax.ShapeDtypeStruct((bsz, s, d), BF16),
                   jax.ShapeDtypeStruct((TOP_K, bsz * s), I32), jax.ShapeDtypeStruct((TOP_K, bsz * s), F32),
                   jax.ShapeDtypeStruct((N_EXPERTS, 1), F32)],
        compiler_params=_cparams(2),
        name="mix",
    )(ya, yb, ga, gb, x, mod, wa, wb, wo, gpm, gpf, wrt_hi, wrt_lo, rbias)


POS_TM = 512


DSP_UNROLL = 8


DSP_CHUNK = 256


def _dispatch_kernel(idx_ref, startrow_ref, tri_ref, low_ref, h_ref, fill_ref, lpos_ref, meta_ref, xs_ref,
                     carryrow_ref, zero_ref, stage_ref, meta_smem, fill_smem, sem_p, sem_f, sem_s):
    i = pl.program_id(0)
    tm = idx_ref.shape[1]
    n_fill = fill_ref.shape[2]
    n_rows = TOP_K * tm
    buf = i % 2
    stage = stage_ref.at[buf]

    def stage_wait(b):
        pltpu.make_async_copy(stage_ref.at[b], xs_ref.at[pl.ds(0, n_rows * ROW_TILE), :], sem_s.at[b]).wait()

    @pl.when(i == 0)
    def _():
        carryrow_ref[...] = jnp.zeros_like(carryrow_ref)
        zero_ref[...] = jnp.zeros_like(zero_ref)

    @pl.when(i >= 2)
    def _():
        stage_wait(buf)

    fill_to_smem = pltpu.make_async_copy(fill_ref.at[0], fill_smem, sem_p)
    fill_to_smem.start()
    fill_to_smem.wait()

    def issue_fill(r8, carry):
        for j in range(8):
            dst0 = pl.multiple_of(fill_smem[0, r8 * 8 + j] * ROW_TILE, ROW_TILE)
            pltpu.make_async_copy(zero_ref, xs_ref.at[pl.ds(dst0, ROW_TILE), :], sem_f).start(priority=j % 2)
        return carry

    lax.fori_loop(0, n_fill // 8, issue_fill, 0)

    row = lax.broadcasted_iota(I32, (N_EXPERTS, tm), 0)
    hits = [row == idx_ref[k:k + 1, :] for k in range(TOP_K)]
    sel = hits[0]
    for h in hits[1:]:
        sel = sel | h
    selb = sel.astype(BF16)
    rank = _dot(selb, tri_ref[...])
    below = _dot(low_ref[...], selb)
    local = rank + jnp.sum(below, axis=1, keepdims=True)
    local_rows = [jnp.sum(jnp.where(h, local, 0.0), axis=0, keepdims=True) for h in hits]
    for k in range(TOP_K):
        lpos_ref[k:k + 1, :] = local_rows[k]
    ones = jnp.ones((8, tm), BF16)
    count_row = _dot_t(ones, selb)[0:1]
    meta_ref[0] = jnp.zeros(meta_ref.shape[1:], I32)
    meta_ref[0, 0:1, :] = count_row.astype(I32)
    meta_ref[0, 1:2, :] = _dot_t(ones, below.astype(BF16))[0:1].astype(I32)
    meta_ref[0, 2:3, :] = (carryrow_ref[...] + startrow_ref[...]).astype(I32)
    carryrow_ref[...] += count_row
    meta_to_smem = pltpu.make_async_copy(meta_ref.at[0], meta_smem, sem_p)
    meta_to_smem.start()

    hb = h_ref[...]
    want = lax.broadcasted_iota(I32, (DSP_CHUNK, tm), 0).astype(F32).astype(BF16)
    one, zero = jnp.ones((), BF16), jnp.zeros((), BF16)
    for c in range(n_rows // DSP_CHUNK):
        lo = float(c * DSP_CHUNK)
        rel = [jnp.where((r >= lo) & (r < lo + DSP_CHUNK), r - lo, -1.0).astype(BF16) for r in local_rows]
        hot = rel[0] == want
        for r in rel[1:]:
            hot = hot | (r == want)
        rows = _dot(jnp.where(hot, one, zero), hb)
        _store_row_tiles(stage.at[pl.ds(c * DSP_CHUNK * ROW_TILE, DSP_CHUNK * ROW_TILE), :],
                         rows, DSP_CHUNK, exact=True)
    meta_to_smem.wait()

    def run(e, carry):
        n = meta_smem[0, e] * ROW_TILE

        @pl.when(n > 0)
        def _():
            src0 = pl.multiple_of(meta_smem[1, e] * ROW_TILE, ROW_TILE)
            dst0 = pl.multiple_of(meta_smem[2, e] * ROW_TILE, ROW_TILE)
            pltpu.make_async_copy(stage.at[pl.ds(src0, n), :], xs_ref.at[pl.ds(dst0, n), :],
                                  sem_s.at[buf]).start()
        return carry

    lax.fori_loop(0, N_EXPERTS, run, 0)
    for n in [n_rows] * (n_fill // n_rows) + [n_fill % n_rows]:
        if n:
            pltpu.make_async_copy(stage_ref.at[0, pl.ds(0, n * ROW_TILE), :],
                                  xs_ref.at[pl.ds(0, n * ROW_TILE), :], sem_f).wait()

    @pl.when(i == pl.num_programs(0) - 1)
    def _():
        stage_wait(buf)

        @pl.when(i >= 1)
        def _():
            stage_wait(1 - buf)


def _dispatch(idx, start, tri, low, h2b, fill_pos):
    t, d = h2b.shape
    tm = tri.shape[0]
    n_steps, _, n_fill = fill_pos.shape
    n_rows = t * TOP_K + n_steps * n_fill
    return pl.pallas_call(
        _dispatch_kernel,
        grid=(n_steps,),
        in_specs=[pl.BlockSpec((TOP_K, tm), lambda i: (0, i)),
                  pl.BlockSpec((1, N_EXPERTS), lambda i: (0, 0)),
                  pl.BlockSpec((tm, tm), lambda i: (0, 0)),
                  pl.BlockSpec((N_EXPERTS, N_EXPERTS), lambda i: (0, 0)),
                  pl.BlockSpec((tm, d), lambda i: (i, 0)),
                  pl.BlockSpec((1, 1, n_fill), lambda i: (i, 0, 0))],
        out_specs=[pl.BlockSpec((TOP_K, tm), lambda i: (0, i)),
                   pl.BlockSpec((1, 8, N_EXPERTS), lambda i: (i, 0, 0)),
                   pl.BlockSpec(memory_space=pl.ANY)],
        out_shape=[jax.ShapeDtypeStruct((TOP_K, t), F32),
                   jax.ShapeDtypeStruct((n_steps, 8, N_EXPERTS), I32),
                   jax.ShapeDtypeStruct((n_rows * ROW_TILE, LANES), U32)],
        scratch_shapes=[pltpu.VMEM((1, N_EXPERTS), F32),
                        pltpu.VMEM((ROW_TILE, LANES), U32),
                        pltpu.VMEM((2, TOP_K * tm * ROW_TILE, LANES), U32),
                        pltpu.SMEM((8, N_EXPERTS), I32), pltpu.SMEM((1, n_fill), I32),
                        pltpu.SemaphoreType.DMA, pltpu.SemaphoreType.DMA, pltpu.SemaphoreType.DMA((2,))],
        compiler_params=_cparams(1),
        name="dispatch",
    )(idx, start[None, :], tri, low, h2b, fill_pos)


EXP_RING = 4


def _experts_kernel(b0_ref, nb_ref, xs_ref, wg_ref, wu_ref, wd_ref, y_ref,
                    xbuf, ybuf, hid_ref, wg_b, wu_b, wd_b, sem_x, sem_y):
    e = pl.program_id(0)
    n_exp = pl.num_programs(0)
    n_used = b0_ref[n_exp - 1] + nb_ref[n_exp - 1]
    last = n_used - 1
    blk_rows = MOE_BLOCK * ROW_TILE

    def rows_of(ref, g):
        return ref.at[pl.ds(pl.multiple_of(g * blk_rows, blk_rows), blk_rows), :]

    def in_copy(g):
        return pltpu.make_async_copy(rows_of(xs_ref, jnp.minimum(g, last)), xbuf.at[g % EXP_RING],
                                     sem_x.at[g % EXP_RING])

    def out_copy(g, s):
        return pltpu.make_async_copy(ybuf.at[s], rows_of(y_ref, g), sem_y.at[s])

    def hidden(g):
        xb = _load_row_tiles(xbuf.at[g % EXP_RING], MOE_BLOCK).astype(BF16)
        return (_silu(_dot(xb, wg_b[...])) * _dot(xb, wu_b[...])).astype(BF16)

    @pl.when(e == 0)
    def _():
        for g in range(EXP_RING - 1):
            in_copy(g).start()

    wg_b[...] = wg_ref[0].astype(BF16)
    wu_b[...] = wu_ref[0].astype(BF16)
    wd_b[...] = wd_ref[0].astype(BF16)
    b0 = b0_ref[e]
    nb = nb_ref[e]

    @pl.when(nb > 0)
    def _():
        in_copy(b0).wait()
        hid_ref[...] = hidden(b0)

    def block(j, carry):
        g = b0 + j
        sy = g % 2

        @pl.when(g >= 2)
        def _():
            out_copy(g - 2, sy).wait()

        @pl.when(j + 1 < nb)
        def _():
            in_copy(g + 1).wait()

        in_copy(g + EXP_RING - 1).start()
        hid_next = hidden(jnp.minimum(g + 1, b0 + nb - 1))
        _store_row_tiles(ybuf.at[sy], _dot(hid_ref[...], wd_b[...]), MOE_BLOCK)
        out_copy(g, sy).start()
        hid_ref[...] = hid_next
        return carry

    lax.fori_loop(0, nb, block, 0)

    @pl.when(e == n_exp - 1)
    def _():
        for i in range(EXP_RING - 1):
            in_copy(n_used + i).wait()
        out_copy(last, last % 2).wait()

        @pl.when(n_used >= 2)
        def _():
            out_copy(last - 1, n_used % 2).wait()

        n_blocks = y_ref.shape[0] // blk_rows
        ybuf[0] = jnp.zeros((blk_rows, LANES), U32)

        def fill(g, carry):
            out_copy(g, 0).start()
            return carry

        def fill_wait(g, carry):
            out_copy(g, 0).wait()
            return carry

        lax.fori_loop(n_used, n_blocks, fill, 0)
        lax.fori_loop(n_used, n_blocks, fill_wait, 0)


def _experts(blk_start, blk_count, xs, wg, wu, wd):
    n_exp, d, e_dim = wg.shape
    blk_rows = MOE_BLOCK * ROW_TILE
    grid_spec = pltpu.PrefetchScalarGridSpec(
        num_scalar_prefetch=2,
        grid=(n_exp,),
        in_specs=[pl.BlockSpec(memory_space=pl.ANY),
                  pl.BlockSpec((1, d, e_dim), lambda e, b0, nb: (e, 0, 0)),
                  pl.BlockSpec((1, d, e_dim), lambda e, b0, nb: (e, 0, 0)),
                  pl.BlockSpec((1, e_dim, d), lambda e, b0, nb: (e, 0, 0))],
        out_specs=pl.BlockSpec(memory_space=pl.ANY),
        scratch_shapes=[pltpu.VMEM((EXP_RING, blk_rows, LANES), U32),
                        pltpu.VMEM((2, blk_rows, LANES), U32),
                        pltpu.VMEM((MOE_BLOCK, e_dim), BF16),
                        pltpu.VMEM((d, e_dim), BF16), pltpu.VMEM((d, e_dim), BF16),
                        pltpu.VMEM((e_dim, d), BF16),
                        pltpu.SemaphoreType.DMA((EXP_RING,)), pltpu.SemaphoreType.DMA((2,))],
    )
    return pl.pallas_call(
        _experts_kernel,
        grid_spec=grid_spec,
        out_shape=jax.ShapeDtypeStruct(xs.shape, U32),
        compiler_params=_cparams(1),
        name="experts",
    )(blk_start, blk_count, xs, wg, wu, wd)


def _final_kernel(meta0_ref, meta1_ref, ys_ref, lpos_ref, wt_ref, h2_ref, x1_ref, mod_ref,
                  wgs_ref, wus_ref, wds_ref, g_ref, o_ref, meta_smem, stage_ref, sem_m, sem_g):
    b, i = pl.program_id(0), pl.program_id(1)
    nt = pl.num_programs(1)
    step = b * nt + i
    last = pl.num_programs(0) * nt - 1
    cur = step % 2
    nxt = 1 - cur
    tm = o_ref.shape[1]
    n_rows = TOP_K * tm

    def issue_runs(meta_ref, s):
        to_smem = pltpu.make_async_copy(meta_ref.at[0], meta_smem, sem_m)
        to_smem.start()
        to_smem.wait()

        def run(e, carry):
            n = meta_smem[0, e] * ROW_TILE

            @pl.when(n > 0)
            def _():
                dst0 = pl.multiple_of(meta_smem[1, e] * ROW_TILE, ROW_TILE)
                src0 = pl.multiple_of(meta_smem[2, e] * ROW_TILE, ROW_TILE)
                pltpu.make_async_copy(ys_ref.at[pl.ds(src0, n), :], stage_ref.at[s, pl.ds(dst0, n), :],
                                      sem_g.at[s]).start()
            return carry

        lax.fori_loop(0, N_EXPERTS, run, 0)

    def wait_runs(s):
        pltpu.make_async_copy(ys_ref.at[pl.ds(0, n_rows * ROW_TILE), :], stage_ref.at[s], sem_g.at[s]).wait()

    @pl.when(step == 0)
    def _():
        issue_runs(meta0_ref, 0)

    issue_runs(meta1_ref, nxt)
    hb = h2_ref[0]
    y = _dot((_silu(_dot(hb, wgs_ref[...])) * _dot(hb, wus_ref[...])).astype(BF16), wds_ref[...])
    wait_runs(cur)
    want = lax.broadcasted_iota(I32, (DSP_CHUNK, tm), 0).astype(F32).astype(BF16)
    lpos = [lpos_ref[k:k + 1, :] for k in range(TOP_K)]
    wts = [wt_ref[k:k + 1, :].astype(BF16) for k in range(TOP_K)]
    zero = jnp.zeros((), BF16)
    for c in range(n_rows // DSP_CHUNK):
        lo = float(c * DSP_CHUNK)
        comb = jnp.zeros((DSP_CHUNK, tm), BF16)
        for r, w in zip(lpos, wts):
            rel = jnp.where((r >= lo) & (r < lo + DSP_CHUNK), r - lo, -1.0).astype(BF16)
            comb = jnp.where(rel == want, w, comb)
        rows = _load_row_tiles(stage_ref.at[cur, pl.ds(c * DSP_CHUNK * ROW_TILE, DSP_CHUNK * ROW_TILE), :],
                               DSP_CHUNK).astype(BF16)
        y = y + lax.dot_general(comb, rows, (((0,), (0,)), ((), ())), preferred_element_type=F32)
    o_ref[0] = x1_ref[0] + mod_ref[0, 5:6, :] * _rmsnorm(y, g_ref[...])

    @pl.when(step == last)
    def _():
        wait_runs(nxt)


def _final(meta, ys, lpos, wts, h2, x1, mod, wgs, wus, wds, g):
    bsz, s, d = x1.shape
    tm = min(POS_TM, s)
    nt = s // tm
    n_tiles = bsz * nt
    sd = wgs.shape[1]
    tok = lambda b, i: (b, i, 0)
    flat = lambda b, i: (0, b * nt + i)
    const = lambda b, i: (0, 0)
    return pl.pallas_call(
        _final_kernel,
        grid=(bsz, nt),
        in_specs=[pl.BlockSpec((1, 8, N_EXPERTS), lambda b, i: (0, 0, 0)),
                  pl.BlockSpec((1, 8, N_EXPERTS), lambda b, i: (jnp.minimum(b * nt + i + 1, n_tiles - 1), 0, 0)),
                  pl.BlockSpec(memory_space=pl.ANY),
                  pl.BlockSpec((TOP_K, tm), flat), pl.BlockSpec((TOP_K, tm), flat),
                  pl.BlockSpec((1, tm, d), tok), pl.BlockSpec((1, tm, d), tok),
                  pl.BlockSpec((1, N_MOD, d), lambda b, i: (b, 0, 0)),
                  pl.BlockSpec((d, sd), const), pl.BlockSpec((d, sd), const), pl.BlockSpec((sd, d), const),
                  pl.BlockSpec((1, d), const)],
        out_specs=pl.BlockSpec((1, tm, d), tok),
        out_shape=jax.ShapeDtypeStruct((bsz, s, d), F32),
        scratch_shapes=[pltpu.SMEM((8, N_EXPERTS), I32),
                        pltpu.VMEM((2, TOP_K * tm * ROW_TILE, LANES), U32),
                        pltpu.SemaphoreType.DMA, pltpu.SemaphoreType.DMA((2,))],
        compiler_params=_cparams(2),
        name="final",
    )(meta, meta, ys, lpos, wts, h2, x1, mod, wgs, wus, wds, g)


def _strict_lower(n):
    r = np.arange(n)
    return r[None, :] < r[:, None]


def kernel(x, c, w_ada, b_ada, g_pre_mix, g_post_mix, w_in, attn_sinks, rel_bias, w_branch_a, w_branch_b,
           w_out, g_pre_ffn, g_post_ffn, w_router, router_bias, w_gate_e, w_up_e, w_down_e,
           w_gate_s, w_up_s, w_down_s):
    bsz, s, d = x.shape
    t = bsz * s
    depth = w_ada.shape[0]
    bucket_hot = (_t5_buckets()[None] == np.arange(NUM_BUCKETS)[:, None, None]).astype(np.float32)
    bias_tab = jnp.sum(bucket_hot[:, None] * rel_bias.astype(F32)[:, :, None, None], axis=0)
    group = SWA_HEADS // SWA_KV_HEADS
    bias_tab = jnp.stack([jnp.concatenate([bias_tab[group * kv + r], bias_tab[group * kv + 2 + r]], axis=0)
                          for kv in range(SWA_KV_HEADS) for r in range(2)])
    sb_tri = jnp.asarray(_strict_lower(SB_T), BF16)
    pos_tri = jnp.asarray(_strict_lower(min(POS_TM, s)).T, BF16)
    expert_low = jnp.asarray(_strict_lower(N_EXPERTS), BF16)
    for l in range(depth):
        wa_hi, wa_lo = _split(w_ada[l])
        mod = _ada(c, wa_hi, wa_lo, b_ada[l][None, :]).reshape(bsz, N_MOD, d)
        qa, ka, va, qb, kb, vb, ga, gb = _inproj(x, mod, g_pre_mix[l][None, :], w_in[l].astype(BF16))
        ya = _swa(qa, ka, va, bias_tab, attn_sinks[l])
        yb = _sb(qb, kb, vb, sb_tri)
        wrt_hi, wrt_lo = _split(w_router[l].T)
        x1, h2b, idx, wts, counts = _mix(
            ya, yb, ga, gb, x, mod, w_branch_a[l].astype(BF16), w_branch_b[l].astype(BF16),
            w_out[l].astype(BF16), g_post_mix[l][None, :], g_pre_ffn[l][None, :],
            wrt_hi, wrt_lo, router_bias[l][:, None])
        counts = counts[:, 0].astype(I32)
        pad_counts = (counts + MOE_BLOCK - 1) // MOE_BLOCK * MOE_BLOCK
        pad_end = jnp.cumsum(pad_counts)
        pad_start = pad_end - pad_counts
        n_blocks = -(-(t * TOP_K) // MOE_BLOCK) + N_EXPERTS
        n_fill = n_blocks * MOE_BLOCK - t * TOP_K
        pad_n = pad_counts - counts
        pad_cum = jnp.cumsum(pad_n) - pad_n
        q = jnp.arange(n_fill, dtype=I32)
        own = (q[:, None] >= pad_cum[None, :]) & (q[:, None] < (pad_cum + pad_n)[None, :])
        inside = jnp.sum(jnp.where(own, (pad_start + counts - pad_cum)[None, :] + q[:, None], 0), axis=1)
        fill_pos = jnp.where(q < jnp.sum(pad_n), inside, pad_end[-1] + q - jnp.sum(pad_n))
        n_steps = t // min(POS_TM, s)
        lpos, meta, xs = _dispatch(idx, pad_start.astype(F32), pos_tri, expert_low, h2b.reshape(t, d),
                                   fill_pos.astype(I32).reshape(n_steps, 1, n_fill // n_steps))
        ys = _experts((pad_start // MOE_BLOCK).astype(I32), (pad_counts // MOE_BLOCK).astype(I32),
                      xs, w_gate_e[l], w_up_e[l], w_down_e[l])
        x = _final(meta, ys, lpos, wts, h2b, x1, mod, w_gate_s[l].astype(BF16), w_up_s[l].astype(BF16),
                   w_down_s[l].astype(BF16), g_post_ffn[l][None, :])
    return x
```

```python
import functools

import numpy as np
import jax
import jax.numpy as jnp
from jax import lax
from jax.experimental import pallas as pl
from jax.experimental.pallas import tpu as pltpu

F32 = jnp.float32
BF16 = jnp.bfloat16
I32 = jnp.int32

D_MODEL = 1024
CHUNK = 64
HEAD_DIM = 64
SWA_HEADS = 8
SWA_KV_HEADS = 2
SWA_BLOCK = 128
WINDOW_CHUNKS = 2
SB_HEADS = 8
SB_BLOCK = 128
NUM_BUCKETS = 32
MAX_DISTANCE = 128
N_EXPERTS = 256
TOP_K = 8
N_GROUPS = 8
GROUP_SIZE = N_EXPERTS // N_GROUPS
TOPK_GROUPS = 4
EXPERT_DIM = 256
ROUTED_SCALE = 2.5
MOE_BLOCK = 128
RMS_EPS = 1e-6
N_MOD = 6
NEG_INF = -1e30

QA_W = SWA_HEADS * HEAD_DIM
KVA_W = SWA_KV_HEADS * HEAD_DIM
QB_W = SB_HEADS * HEAD_DIM
IN_WIDTH = QA_W + 2 * KVA_W + 3 * QB_W + 2 * D_MODEL
LANES = 128
HEAD_PAIRS = SB_HEADS // 2
Q_SCALE = HEAD_DIM ** -0.5
LOG2_E = 1.4426950408889634

VMEM_LIMIT = 56 * 1024 * 1024


def _cparams(n_axes, vmem=VMEM_LIMIT):
    return pltpu.CompilerParams(dimension_semantics=("arbitrary",) * n_axes, vmem_limit_bytes=vmem)


def _dot(a, b):
    return jnp.dot(a, b, preferred_element_type=F32)


def _dot_t(a, b):
    return lax.dot_general(a, b, (((1,), (1,)), ((), ())), preferred_element_type=F32)


def _split(x):
    hi = x.astype(BF16)
    lo = (x - hi.astype(F32)).astype(BF16)
    return hi, lo


def _dot3(a, b_hi, b_lo):
    a_hi, a_lo = _split(a)
    return _dot(a_hi, b_hi) + (_dot(a_hi, b_lo) + _dot(a_lo, b_hi))


def _rmsnorm(x, g):
    return x * lax.rsqrt(jnp.mean(x * x, axis=-1, keepdims=True) + RMS_EPS) * g


def _silu(x):
    return x * jax.nn.sigmoid(x)


U32 = jnp.uint32
ROW_TILE = D_MODEL // (2 * LANES)
HI_HALF = 0xFFFF0000


def _store_row_tiles(ref, x, n, exact=False):
    half = x.shape[1] // 2
    if exact:
        words = pltpu.bitcast(x[:, half:], U32) | (pltpu.bitcast(x[:, :half], U32) >> 16)
    else:
        lo = pltpu.bitcast(x[:, :half].astype(BF16).astype(F32), U32) >> 16
        hi = pltpu.bitcast(x[:, half:].astype(BF16).astype(F32), U32) & jnp.uint32(HI_HALF)
        words = hi | lo
    for c in range(ROW_TILE):
        ref[pl.ds(c, n, stride=ROW_TILE), :] = words[:, c * LANES:(c + 1) * LANES]


def _load_row_tiles(ref, n):
    words = jnp.concatenate([ref[pl.ds(c, n, stride=ROW_TILE), :] for c in range(ROW_TILE)], axis=1)
    lo = pltpu.bitcast(words << 16, F32)
    hi = pltpu.bitcast(words & jnp.uint32(HI_HALF), F32)
    return jnp.concatenate([lo, hi], axis=1)


def _ada_kernel(c_ref, wh_ref, wl_ref, b_ref, o_ref):
    o_ref[...] = _dot3(_silu(c_ref[...]), wh_ref[...], wl_ref[...]) + b_ref[...]


def _ada(c, w_hi, w_lo, b):
    bsz, d = c.shape
    n = w_hi.shape[1] // d
    return pl.pallas_call(
        _ada_kernel,
        grid=(n,),
        in_specs=[pl.BlockSpec((bsz, d), lambda j: (0, 0)),
                  pl.BlockSpec((d, d), lambda j: (0, j)),
                  pl.BlockSpec((d, d), lambda j: (0, j)),
                  pl.BlockSpec((1, d), lambda j: (0, j))],
        out_specs=pl.BlockSpec((bsz, d), lambda j: (0, j)),
        out_shape=jax.ShapeDtypeStruct((bsz, n * d), F32),
        compiler_params=_cparams(1),
        name="ada",
    )(c, w_hi, w_lo, b)


INPROJ_TM = 512


def _inproj_kernel(x_ref, mod_ref, g_ref, w_ref,
                   qa_ref, ka_ref, va_ref, qb_ref, kb_ref, vb_ref, ga_ref, gb_ref):
    x = x_ref[0]
    h = _rmsnorm(x, g_ref[...]) * (1.0 + mod_ref[0, 1:2, :]) + mod_ref[0, 0:1, :]
    hb = h.astype(BF16)
    tm = x.shape[0]
    lo_half = lax.broadcasted_iota(I32, (tm, LANES), 1) < HEAD_DIM

    def proj(c0, n):
        return _dot(hb, w_ref[:, c0:c0 + n])

    def put_q(dst, base, scale):
        for c in range(0, QA_W, 256):
            dst[0, :, c:c + 256] = (proj(base + c, 256) * scale).astype(BF16)

    put_q(qa_ref, 0, Q_SCALE)
    r = proj(QA_W, 2 * KVA_W)
    for src, dst in ((r[:, :LANES], ka_ref), (r[:, LANES:], va_ref)):
        rolled = pltpu.roll(src, HEAD_DIM, axis=1)
        dst[0, :, 0 * LANES:1 * LANES] = jnp.where(lo_half, src, 0.0).astype(BF16)
        dst[0, :, 1 * LANES:2 * LANES] = jnp.where(lo_half, 0.0, rolled).astype(BF16)
        dst[0, :, 2 * LANES:3 * LANES] = jnp.where(lo_half, rolled, 0.0).astype(BF16)
        dst[0, :, 3 * LANES:4 * LANES] = jnp.where(lo_half, 0.0, src).astype(BF16)
    base_qb = QA_W + 2 * KVA_W
    put_q(qb_ref, base_qb, Q_SCALE * LOG2_E)
    for dst, base in ((kb_ref, base_qb + QB_W), (vb_ref, base_qb + 2 * QB_W)):
        for c in range(0, QB_W, 256):
            r = proj(base + c, 256)
            for t in range(2):
                pair = r[:, t * LANES:(t + 1) * LANES]
                o = 2 * (c + t * LANES)
                dst[0, :, o:o + LANES] = jnp.where(lo_half, pair, 0.0).astype(BF16)
                dst[0, :, o + LANES:o + 2 * LANES] = jnp.where(lo_half, 0.0, pair).astype(BF16)
    base_g = base_qb + 3 * QB_W
    for dst, base in ((ga_ref, base_g), (gb_ref, base_g + D_MODEL)):
        for c in range(0, D_MODEL, 256):
            dst[0, :, c:c + 256] = jax.nn.sigmoid(proj(base + c, 256)).astype(BF16)


def _inproj(x, mod, g, w_in):
    bsz, s, d = x.shape
    tm = min(INPROJ_TM, s)
    widths = (QA_W, 4 * LANES, 4 * LANES, QB_W, 2 * QB_W, 2 * QB_W, d, d)
    return pl.pallas_call(
        _inproj_kernel,
        grid=(bsz, s // tm),
        in_specs=[pl.BlockSpec((1, tm, d), lambda b, i: (b, i, 0)),
                  pl.BlockSpec((1, N_MOD, d), lambda b, i: (b, 0, 0)),
                  pl.BlockSpec((1, d), lambda b, i: (0, 0)),
                  pl.BlockSpec((d, IN_WIDTH), lambda b, i: (0, 0))],
        out_specs=[pl.BlockSpec((1, tm, w), lambda b, i: (b, i, 0)) for w in widths],
        out_shape=[jax.ShapeDtypeStruct((bsz, s, w), BF16) for w in widths],
        compiler_params=_cparams(2),
        name="inproj",
    )(x, mod, g, w_in)


def _t5_buckets():
    i = np.arange(SWA_BLOCK)[:, None]
    j = np.arange(2 * SWA_BLOCK)[None, :]
    rel = (j - SWA_BLOCK) - i
    nb = NUM_BUCKETS // 2
    bucket = (rel > 0).astype(np.int32) * nb
    n = np.abs(rel)
    max_exact = nb // 2
    large = max_exact + (np.log(np.maximum(n, 1) / max_exact)
                         / np.log(MAX_DISTANCE / max_exact) * (nb - max_exact)).astype(np.int32)
    large = np.minimum(large, nb - 1)
    return (bucket + np.where(n < max_exact, n, large)).astype(np.int32)


def _swa_kernel(sink_ref, q_ref, kp_ref, kc_ref, vp_ref, vc_ref, bias_ref, o_ref):
    n = pl.program_id(1)
    shape = (2 * SWA_BLOCK, 2 * SWA_BLOCK)
    row = lax.broadcasted_iota(I32, shape, 0)
    row_hi = (row % SWA_BLOCK) // CHUNK
    col = lax.broadcasted_iota(I32, shape, 1)
    col_chunk = col // CHUNK
    valid = (col_chunk >= row_hi) & (col_chunk <= row_hi + WINDOW_CHUNKS)
    valid = valid & ((n > 0) | (col >= SWA_BLOCK))
    first_head = lax.broadcasted_iota(I32, (2 * SWA_BLOCK, 1), 0) < SWA_BLOCK
    group = SWA_HEADS // SWA_KV_HEADS
    for kv in range(SWA_KV_HEADS):
        q = jnp.concatenate([q_ref[0, :, (2 * kv) * LANES:(2 * kv + 1) * LANES],
                             q_ref[0, :, (2 * kv + 1) * LANES:(2 * kv + 2) * LANES]], axis=0)
        acc = jnp.zeros((2 * SWA_BLOCK, LANES), F32)
        for r in range(2):
            slot = 2 * kv + r
            sl = slice(slot * LANES, (slot + 1) * LANES)
            kcat = jnp.concatenate([kp_ref[0, :, sl], kc_ref[0, :, sl]], axis=0)
            logits = jnp.where(valid, _dot_t(q, kcat) + bias_ref[slot], NEG_INF)
            sink = jnp.where(first_head, sink_ref[group * kv + r], sink_ref[group * kv + 2 + r])
            m = jnp.maximum(jnp.max(logits, axis=-1, keepdims=True), sink)
            e = jnp.exp(logits - m)
            den = jnp.sum(e, axis=-1, keepdims=True) + jnp.exp(sink - m)
            probs = (e / den).astype(BF16)
            vcat = jnp.concatenate([vp_ref[0, :, sl], vc_ref[0, :, sl]], axis=0)
            acc = acc + _dot(probs, vcat)
        o_ref[0, :, (2 * kv) * LANES:(2 * kv + 1) * LANES] = acc[:SWA_BLOCK].astype(BF16)
        o_ref[0, :, (2 * kv + 1) * LANES:(2 * kv + 2) * LANES] = acc[SWA_BLOCK:].astype(BF16)


def _swa(qa, ka, va, bias, sinks):
    bsz, s, _ = qa.shape
    nb = s // SWA_BLOCK
    cur = lambda b, n: (b, n, 0)
    prev = lambda b, n: (b, jnp.maximum(n - 1, 0), 0)
    blk = (1, SWA_BLOCK, 4 * LANES)
    return pl.pallas_call(
        _swa_kernel,
        grid=(bsz, nb),
        in_specs=[pl.BlockSpec(memory_space=pltpu.SMEM),
                  pl.BlockSpec(blk, cur),
                  pl.BlockSpec(blk, prev), pl.BlockSpec(blk, cur),
                  pl.BlockSpec(blk, prev), pl.BlockSpec(blk, cur),
                  pl.BlockSpec((2 * SWA_KV_HEADS, 2 * SWA_BLOCK, 2 * SWA_BLOCK), lambda b, n: (0, 0, 0))],
        out_specs=pl.BlockSpec(blk, cur),
        out_shape=jax.ShapeDtypeStruct((bsz, s, QA_W), BF16),
        compiler_params=_cparams(2),
        name="swa",
    )(sinks, qa, ka, ka, va, va, bias)


SB_T = 2 * SB_BLOCK


def _sb_kernel(q_ref, k_ref, v_ref, tri_ref, o_ref, acc_ref, c_ref):
    i = pl.program_id(1)
    tri = tri_ref[...]
    shape = (SB_T, SB_T)
    causal = lax.broadcasted_iota(I32, shape, 1) < lax.broadcasted_iota(I32, shape, 0)
    acc_ref[...] = jnp.zeros_like(acc_ref)
    c_ref[...] = jnp.zeros_like(c_ref)

    def stack(ref, r0, p):
        t = ref[0, pl.ds(r0, SB_T), 2 * p * LANES:(2 * p + 2) * LANES]
        return jnp.concatenate([t[:, :LANES], t[:, LANES:]], axis=0)

    def sweep(j, diag):
        r0 = pl.multiple_of(j * SB_T, SB_T)
        for p in range(HEAD_PAIRS):
            q = q_ref[0, :, p * LANES:(p + 1) * LANES]
            z2 = _dot_t(q, stack(k_ref, r0, p))
            ws = []
            for r in range(2):
                h = 2 * p + r
                z = z2[:, r * SB_T:(r + 1) * SB_T]
                nlk = jnp.maximum(z, 0.0) + jnp.log2(1.0 + jnp.exp2(-jnp.abs(z)))
                if diag:
                    nlk = jnp.where(causal, nlk, 0.0)
                rest = _dot(nlk.astype(BF16), tri)
                c = c_ref[h]
                lw = z - nlk - rest
                w = jnp.concatenate([jnp.exp2(lw[:, :LANES] - c), jnp.exp2(lw[:, LANES:] - c)], axis=1)
                if diag:
                    w = jnp.where(causal, w, 0.0)
                ws.append(w.astype(BF16))
                c_ref[h] = c + jnp.sum(nlk, axis=-1, keepdims=True)
            acc_ref[p] += _dot(jnp.concatenate(ws, axis=1), stack(v_ref, r0, p))

    sweep(i, True)

    def body(t, carry):
        sweep(i - 1 - t, False)
        return carry

    lax.fori_loop(0, i, body, 0)
    for p in range(HEAD_PAIRS):
        o_ref[0, :, p * LANES:(p + 1) * LANES] = acc_ref[p].astype(BF16)


def _sb(qb, kb, vb, tri):
    bsz, s, _ = qb.shape
    return pl.pallas_call(
        _sb_kernel,
        grid=(bsz, s // SB_T),
        in_specs=[pl.BlockSpec((1, SB_T, QB_W), lambda b, i: (b, i, 0)),
                  pl.BlockSpec((1, s, 2 * QB_W), lambda b, i: (b, 0, 0)),
                  pl.BlockSpec((1, s, 2 * QB_W), lambda b, i: (b, 0, 0)),
                  pl.BlockSpec((SB_T, SB_T), lambda b, i: (0, 0))],
        out_specs=pl.BlockSpec((1, SB_T, QB_W), lambda b, i: (b, i, 0)),
        out_shape=jax.ShapeDtypeStruct((bsz, s, QB_W), BF16),
        scratch_shapes=[pltpu.VMEM((HEAD_PAIRS, SB_T, LANES), F32),
                        pltpu.VMEM((SB_HEADS, SB_T, LANES), F32)],
        compiler_params=_cparams(2),
        name="sb",
    )(qb, kb, vb, tri)


MIX_TM = 512


def _route_t(scores, bias):
    neg = -jnp.inf
    n_e, tm = scores.shape
    choice = scores + bias
    group_rows = [choice[g * GROUP_SIZE:(g + 1) * GROUP_SIZE, :] for g in range(N_GROUPS)]
    gscore = []
    for rows in group_rows:
        m1 = jnp.max(rows, axis=0, keepdims=True)
        top = rows == m1
        n_top = jnp.sum(top.astype(F32), axis=0, keepdims=True)
        m2 = jnp.max(jnp.where(top, neg, rows), axis=0, keepdims=True)
        gscore.append(m1 + jnp.where(n_top >= 2.0, m1, m2))
    parts = []
    for g, rows in enumerate(group_rows):
        rank = jnp.zeros((1, tm), F32)
        for o in range(N_GROUPS):
            if o != g:
                beats = (gscore[o] >= gscore[g]) if o < g else (gscore[o] > gscore[g])
                rank = rank + beats.astype(F32)
        parts.append(jnp.where(rank < float(TOPK_GROUPS), rows, neg))
    masked = jnp.concatenate(parts, axis=0)
    row = lax.broadcasted_iota(I32, (n_e, tm), 0).astype(F32)
    idx_rows, w_rows = [], []
    sel = jnp.zeros((n_e, tm), jnp.bool_)
    for _ in range(TOP_K):
        m = jnp.max(masked, axis=0, keepdims=True)
        first = jnp.min(jnp.where(masked == m, row, float(n_e)), axis=0, keepdims=True)
        hit = row == first
        idx_rows.append(first)
        w_rows.append(jnp.sum(jnp.where(hit, scores, 0.0), axis=0, keepdims=True))
        masked = jnp.where(hit, neg, masked)
        sel = sel | hit
    return idx_rows, w_rows, sel


def _mix_kernel(ya_ref, yb_ref, ga_ref, gb_ref, x_ref, mod_ref, wa_ref, wb_ref, wo_ref,
                gpm_ref, gpf_ref, wrh_ref, wrl_ref, rb_ref,
                x1_ref, h2b_ref, idx_ref, wt_ref, cnt_ref):
    first = (pl.program_id(0) == 0) & (pl.program_id(1) == 0)
    merged = (ga_ref[0].astype(F32) * _dot(ya_ref[0], wa_ref[...])
              + gb_ref[0].astype(F32) * _dot(yb_ref[0], wb_ref[...]))
    o = _dot(merged.astype(BF16), wo_ref[...])
    x1 = x_ref[0] + mod_ref[0, 2:3, :] * _rmsnorm(o, gpm_ref[...])
    h2 = _rmsnorm(x1, gpf_ref[...]) * (1.0 + mod_ref[0, 4:5, :]) + mod_ref[0, 3:4, :]
    x1_ref[0] = x1
    h2b_ref[0] = h2.astype(BF16)
    h_hi, h_lo = _split(h2)
    w_hi = wrh_ref[...]
    logits = _dot_t(w_hi, h_hi) + (_dot_t(w_hi, h_lo) + _dot_t(wrl_ref[...], h_hi))
    idx_rows, w_rows, sel = _route_t(jax.nn.sigmoid(logits), rb_ref[...])
    wsum = w_rows[0]
    for r in w_rows[1:]:
        wsum = wsum + r
    for k in range(TOP_K):
        idx_ref[k:k + 1, :] = idx_rows[k].astype(I32)
        wt_ref[k:k + 1, :] = w_rows[k] / wsum * ROUTED_SCALE

    @pl.when(first)
    def _():
        cnt_ref[...] = jnp.zeros_like(cnt_ref)

    cnt_ref[...] += jnp.sum(sel.astype(F32), axis=1, keepdims=True)


def _mix(ya, yb, ga, gb, x, mod, wa, wb, wo, gpm, gpf, wrt_hi, wrt_lo, rbias):
    bsz, s, d = x.shape
    tm = min(MIX_TM, s)
    nt = s // tm
    tok = lambda b, i: (b, i, 0)
    flat = lambda b, i: (0, b * nt + i)
    const = lambda b, i: (0, 0)
    return pl.pallas_call(
        _mix_kernel,
        grid=(bsz, s // tm),
        in_specs=[pl.BlockSpec((1, tm, QA_W), tok), pl.BlockSpec((1, tm, QB_W), tok),
                  pl.BlockSpec((1, tm, d), tok), pl.BlockSpec((1, tm, d), tok),
                  pl.BlockSpec((1, tm, d), tok),
                  pl.BlockSpec((1, N_MOD, d), lambda b, i: (b, 0, 0)),
                  pl.BlockSpec((QA_W, d), const), pl.BlockSpec((QB_W, d), const),
                  pl.BlockSpec((d, d), const),
                  pl.BlockSpec((1, d), const), pl.BlockSpec((1, d), const),
                  pl.BlockSpec((N_EXPERTS, d), const), pl.BlockSpec((N_EXPERTS, d), const),
                  pl.BlockSpec((N_EXPERTS, 1), const)],
        out_specs=[pl.BlockSpec((1, tm, d), tok),
                   pl.BlockSpec((1, tm, d), tok),
                   pl.BlockSpec((TOP_K, tm), flat), pl.BlockSpec((TOP_K, tm), flat),
                   pl.BlockSpec((N_EXPERTS, 1), const)],
        out_shape=[jax.ShapeDtypeStruct((bsz, s, d), F32),
                   jax.ShapeDtypeStruct((bsz, s, d), BF16),
                   jax.ShapeDtypeStruct((TOP_K, bsz * s), I32), jax.ShapeDtypeStruct((TOP_K, bsz * s), F32),
                   jax.ShapeDtypeStruct((N_EXPERTS, 1), F32)],
        compiler_params=_cparams(2),
        name="mix",
    )(ya, yb, ga, gb, x, mod, wa, wb, wo, gpm, gpf, wrt_hi, wrt_lo, rbias)


POS_TM = 512


DSP_UNROLL = 8


DSP_CHUNK = 256


def _dispatch_kernel(idx_ref, startrow_ref, tri_ref, low_ref, h_ref, fill_ref, lpos_ref, meta_ref, xs_ref,
                     carryrow_ref, zero_ref, stage_ref, meta_smem, fill_smem, sem_p, sem_f, sem_s):
    i = pl.program_id(0)
    tm = idx_ref.shape[1]
    n_fill = fill_ref.shape[2]
    n_rows = TOP_K * tm
    buf = i % 2
    stage = stage_ref.at[buf]

    def stage_wait(b):
        pltpu.make_async_copy(stage_ref.at[b], xs_ref.at[pl.ds(0, n_rows * ROW_TILE), :], sem_s.at[b]).wait()

    @pl.when(i == 0)
    def _():
        carryrow_ref[...] = jnp.zeros_like(carryrow_ref)
        zero_ref[...] = jnp.zeros_like(zero_ref)

    @pl.when(i >= 2)
    def _():
        stage_wait(buf)

    fill_to_smem = pltpu.make_async_copy(fill_ref.at[0], fill_smem, sem_p)
    fill_to_smem.start()
    fill_to_smem.wait()

    def issue_fill(r8, carry):
        for j in range(8):
            dst0 = pl.multiple_of(fill_smem[0, r8 * 8 + j] * ROW_TILE, ROW_TILE)
            pltpu.make_async_copy(zero_ref, xs_ref.at[pl.ds(dst0, ROW_TILE), :], sem_f).start(priority=j % 2)
        return carry

    lax.fori_loop(0, n_fill // 8, issue_fill, 0)

    row = lax.broadcasted_iota(I32, (N_EXPERTS, tm), 0)
    hits = [row == idx_ref[k:k + 1, :] for k in range(TOP_K)]
    sel = hits[0]
    for h in hits[1:]:
        sel = sel | h
    selb = sel.astype(BF16)
    rank = _dot(selb, tri_ref[...])
    below = _dot(low_ref[...], selb)
    local = rank + jnp.sum(below, axis=1, keepdims=True)
    local_rows = [jnp.sum(jnp.where(h, local, 0.0), axis=0, keepdims=True) for h in hits]
    for k in range(TOP_K):
        lpos_ref[k:k + 1, :] = local_rows[k]
    ones = jnp.ones((8, tm), BF16)
    count_row = _dot_t(ones, selb)[0:1]
    meta_ref[0] = jnp.zeros(meta_ref.shape[1:], I32)
    meta_ref[0, 0:1, :] = count_row.astype(I32)
    meta_ref[0, 1:2, :] = _dot_t(ones, below.astype(BF16))[0:1].astype(I32)
    meta_ref[0, 2:3, :] = (carryrow_ref[...] + startrow_ref[...]).astype(I32)
    carryrow_ref[...] += count_row
    meta_to_smem = pltpu.make_async_copy(meta_ref.at[0], meta_smem, sem_p)
    meta_to_smem.start()

    hb = h_ref[...]
    want = lax.broadcasted_iota(I32, (DSP_CHUNK, tm), 0).astype(F32).astype(BF16)
    one, zero = jnp.ones((), BF16), jnp.zeros((), BF16)
    for c in range(n_rows // DSP_CHUNK):
        lo = float(c * DSP_CHUNK)
        rel = [jnp.where((r >= lo) & (r < lo + DSP_CHUNK), r - lo, -1.0).astype(BF16) for r in local_rows]
        hot = rel[0] == want
        for r in rel[1:]:
            hot = hot | (r == want)
        rows = _dot(jnp.where(hot, one, zero), hb)
        _store_row_tiles(stage.at[pl.ds(c * DSP_CHUNK * ROW_TILE, DSP_CHUNK * ROW_TILE), :],
                         rows, DSP_CHUNK, exact=True)
    meta_to_smem.wait()

    def run(e2, carry):
        for prio in range(2):
            e = 2 * e2 + prio
            n = meta_smem[0, e] * ROW_TILE

            @pl.when(n > 0)
            def _(e=e, n=n, prio=prio):
                src0 = pl.multiple_of(meta_smem[1, e] * ROW_TILE, ROW_TILE)
                dst0 = pl.multiple_of(meta_smem[2, e] * ROW_TILE, ROW_TILE)
                pltpu.make_async_copy(stage.at[pl.ds(src0, n), :], xs_ref.at[pl.ds(dst0, n), :],
                                      sem_s.at[buf]).start(priority=prio)
        return carry

    lax.fori_loop(0, N_EXPERTS // 2, run, 0)
    for n in [n_rows] * (n_fill // n_rows) + [n_fill % n_rows]:
        if n:
            pltpu.make_async_copy(stage_ref.at[0, pl.ds(0, n * ROW_TILE), :],
                                  xs_ref.at[pl.ds(0, n * ROW_TILE), :], sem_f).wait()

    @pl.when(i == pl.num_programs(0) - 1)
    def _():
        stage_wait(buf)

        @pl.when(i >= 1)
        def _():
            stage_wait(1 - buf)


def _dispatch(idx, start, tri, low, h2b, fill_pos):
    t, d = h2b.shape
    tm = tri.shape[0]
    n_steps, _, n_fill = fill_pos.shape
    n_rows = t * TOP_K + n_steps * n_fill
    return pl.pallas_call(
        _dispatch_kernel,
        grid=(n_steps,),
        in_specs=[pl.BlockSpec((TOP_K, tm), lambda i: (0, i)),
                  pl.BlockSpec((1, N_EXPERTS), lambda i: (0, 0)),
                  pl.BlockSpec((tm, tm), lambda i: (0, 0)),
                  pl.BlockSpec((N_EXPERTS, N_EXPERTS), lambda i: (0, 0)),
                  pl.BlockSpec((tm, d), lambda i: (i, 0)),
                  pl.BlockSpec((1, 1, n_fill), lambda i: (i, 0, 0))],
        out_specs=[pl.BlockSpec((TOP_K, tm), lambda i: (0, i)),
                   pl.BlockSpec((1, 8, N_EXPERTS), lambda i: (i, 0, 0)),
                   pl.BlockSpec(memory_space=pl.ANY)],
        out_shape=[jax.ShapeDtypeStruct((TOP_K, t), F32),
                   jax.ShapeDtypeStruct((n_steps, 8, N_EXPERTS), I32),
                   jax.ShapeDtypeStruct((n_rows * ROW_TILE, LANES), U32)],
        scratch_shapes=[pltpu.VMEM((1, N_EXPERTS), F32),
                        pltpu.VMEM((ROW_TILE, LANES), U32),
                        pltpu.VMEM((2, TOP_K * tm * ROW_TILE, LANES), U32),
                        pltpu.SMEM((8, N_EXPERTS), I32), pltpu.SMEM((1, n_fill), I32),
                        pltpu.SemaphoreType.DMA, pltpu.SemaphoreType.DMA, pltpu.SemaphoreType.DMA((2,))],
        compiler_params=_cparams(1),
        name="dispatch",
    )(idx, start[None, :], tri, low, h2b, fill_pos)


EXP_RING = 4


def _experts_kernel(b0_ref, nb_ref, xs_ref, wg_ref, wu_ref, wd_ref, y_ref,
                    xbuf, ybuf, hid_ref, wg_b, wu_b, wd_b, sem_x, sem_y):
    e = pl.program_id(0)
    n_exp = pl.num_programs(0)
    n_used = b0_ref[n_exp - 1] + nb_ref[n_exp - 1]
    last = n_used - 1
    blk_rows = MOE_BLOCK * ROW_TILE

    def rows_of(ref, g):
        return ref.at[pl.ds(pl.multiple_of(g * blk_rows, blk_rows), blk_rows), :]

    def in_copy(g):
        return pltpu.make_async_copy(rows_of(xs_ref, jnp.minimum(g, last)), xbuf.at[g % EXP_RING],
                                     sem_x.at[g % EXP_RING])

    def out_copy(g, s):
        return pltpu.make_async_copy(ybuf.at[s], rows_of(y_ref, g), sem_y.at[s])

    def hidden(g):
        xb = _load_row_tiles(xbuf.at[g % EXP_RING], MOE_BLOCK).astype(BF16)
        return (_silu(_dot(xb, wg_b[...])) * _dot(xb, wu_b[...])).astype(BF16)

    @pl.when(e == 0)
    def _():
        for g in range(EXP_RING - 1):
            in_copy(g).start()

    wg_b[...] = wg_ref[0].astype(BF16)
    wu_b[...] = wu_ref[0].astype(BF16)
    wd_b[...] = wd_ref[0].astype(BF16)
    b0 = b0_ref[e]
    nb = nb_ref[e]

    @pl.when(nb > 0)
    def _():
        in_copy(b0).wait()
        hid_ref[...] = hidden(b0)

    def block(j, carry):
        g = b0 + j
        sy = g % 2

        @pl.when(g >= 2)
        def _():
            out_copy(g - 2, sy).wait()

        @pl.when(j + 1 < nb)
        def _():
            in_copy(g + 1).wait()

        in_copy(g + EXP_RING - 1).start()
        hid_next = hidden(jnp.minimum(g + 1, b0 + nb - 1))
        _store_row_tiles(ybuf.at[sy], _dot(hid_ref[...], wd_b[...]), MOE_BLOCK)
        out_copy(g, sy).start()
        hid_ref[...] = hid_next
        return carry

    lax.fori_loop(0, nb, block, 0)

    @pl.when(e == n_exp - 1)
    def _():
        for i in range(EXP_RING - 1):
            in_copy(n_used + i).wait()
        out_copy(last, last % 2).wait()

        @pl.when(n_used >= 2)
        def _():
            out_copy(last - 1, n_used % 2).wait()

        n_blocks = y_ref.shape[0] // blk_rows
        ybuf[0] = jnp.zeros((blk_rows, LANES), U32)

        def fill(g, carry):
            out_copy(g, 0).start()
            return carry

        def fill_wait(g, carry):
            out_copy(g, 0).wait()
            return carry

        lax.fori_loop(n_used, n_blocks, fill, 0)
        lax.fori_loop(n_used, n_blocks, fill_wait, 0)


def _experts(blk_start, blk_count, xs, wg, wu, wd):
    n_exp, d, e_dim = wg.shape
    blk_rows = MOE_BLOCK * ROW_TILE
    grid_spec = pltpu.PrefetchScalarGridSpec(
        num_scalar_prefetch=2,
        grid=(n_exp,),
        in_specs=[pl.BlockSpec(memory_space=pl.ANY),
                  pl.BlockSpec((1, d, e_dim), lambda e, b0, nb: (e, 0, 0)),
                  pl.BlockSpec((1, d, e_dim), lambda e, b0, nb: (e, 0, 0)),
                  pl.BlockSpec((1, e_dim, d), lambda e, b0, nb: (e, 0, 0))],
        out_specs=pl.BlockSpec(memory_space=pl.ANY),
        scratch_shapes=[pltpu.VMEM((EXP_RING, blk_rows, LANES), U32),
                        pltpu.VMEM((2, blk_rows, LANES), U32),
                        pltpu.VMEM((MOE_BLOCK, e_dim), BF16),
                        pltpu.VMEM((d, e_dim), BF16), pltpu.VMEM((d, e_dim), BF16),
                        pltpu.VMEM((e_dim, d), BF16),
                        pltpu.SemaphoreType.DMA((EXP_RING,)), pltpu.SemaphoreType.DMA((2,))],
    )
    return pl.pallas_call(
        _experts_kernel,
        grid_spec=grid_spec,
        out_shape=jax.ShapeDtypeStruct(xs.shape, U32),
        compiler_params=_cparams(1),
        name="experts",
    )(blk_start, blk_count, xs, wg, wu, wd)


def _final_kernel(meta0_ref, meta1_ref, ys_ref, lpos_ref, wt_ref, h2_ref, x1_ref, mod_ref,
                  wgs_ref, wus_ref, wds_ref, g_ref, o_ref, meta_smem, stage_ref, sem_m, sem_g):
    b, i = pl.program_id(0), pl.program_id(1)
    nt = pl.num_programs(1)
    step = b * nt + i
    last = pl.num_programs(0) * nt - 1
    cur = step % 2
    nxt = 1 - cur
    tm = o_ref.shape[1]
    n_rows = TOP_K * tm

    def issue_runs(meta_ref, s):
        to_smem = pltpu.make_async_copy(meta_ref.at[0], meta_smem, sem_m)
        to_smem.start()
        to_smem.wait()

        def run(e2, carry):
            for prio in range(2):
                e = 2 * e2 + prio
                n = meta_smem[0, e] * ROW_TILE

                @pl.when(n > 0)
                def _(e=e, n=n, prio=prio):
                    dst0 = pl.multiple_of(meta_smem[1, e] * ROW_TILE, ROW_TILE)
                    src0 = pl.multiple_of(meta_smem[2, e] * ROW_TILE, ROW_TILE)
                    pltpu.make_async_copy(ys_ref.at[pl.ds(src0, n), :], stage_ref.at[s, pl.ds(dst0, n), :],
                                          sem_g.at[s]).start(priority=prio)
            return carry

        lax.fori_loop(0, N_EXPERTS // 2, run, 0)

    def wait_runs(s):
        pltpu.make_async_copy(ys_ref.at[pl.ds(0, n_rows * ROW_TILE), :], stage_ref.at[s], sem_g.at[s]).wait()

    @pl.when(step == 0)
    def _():
        issue_runs(meta0_ref, 0)

    issue_runs(meta1_ref, nxt)
    hb = h2_ref[0]
    y = _dot((_silu(_dot(hb, wgs_ref[...])) * _dot(hb, wus_ref[...])).astype(BF16), wds_ref[...])
    wait_runs(cur)
    want = lax.broadcasted_iota(I32, (DSP_CHUNK, tm), 0).astype(F32).astype(BF16)
    lpos = [lpos_ref[k:k + 1, :] for k in range(TOP_K)]
    wts = [wt_ref[k:k + 1, :].astype(BF16) for k in range(TOP_K)]
    zero = jnp.zeros((), BF16)
    for c in range(n_rows // DSP_CHUNK):
        lo = float(c * DSP_CHUNK)
        comb = jnp.zeros((DSP_CHUNK, tm), BF16)
        for r, w in zip(lpos, wts):
            rel = jnp.where((r >= lo) & (r < lo + DSP_CHUNK), r - lo, -1.0).astype(BF16)
            comb = jnp.where(rel == want, w, comb)
        rows = _load_row_tiles(stage_ref.at[cur, pl.ds(c * DSP_CHUNK * ROW_TILE, DSP_CHUNK * ROW_TILE), :],
                               DSP_CHUNK).astype(BF16)
        y = y + lax.dot_general(comb, rows, (((0,), (0,)), ((), ())), preferred_element_type=F32)
    o_ref[0] = x1_ref[0] + mod_ref[0, 5:6, :] * _rmsnorm(y, g_ref[...])

    @pl.when(step == last)
    def _():
        wait_runs(nxt)


def _final(meta, ys, lpos, wts, h2, x1, mod, wgs, wus, wds, g):
    bsz, s, d = x1.shape
    tm = min(POS_TM, s)
    nt = s // tm
    n_tiles = bsz * nt
    sd = wgs.shape[1]
    tok = lambda b, i: (b, i, 0)
    flat = lambda b, i: (0, b * nt + i)
    const = lambda b, i: (0, 0)
    return pl.pallas_call(
        _final_kernel,
        grid=(bsz, nt),
        in_specs=[pl.BlockSpec((1, 8, N_EXPERTS), lambda b, i: (0, 0, 0)),
                  pl.BlockSpec((1, 8, N_EXPERTS), lambda b, i: (jnp.minimum(b * nt + i + 1, n_tiles - 1), 0, 0)),
                  pl.BlockSpec(memory_space=pl.ANY),
                  pl.BlockSpec((TOP_K, tm), flat), pl.BlockSpec((TOP_K, tm), flat),
                  pl.BlockSpec((1, tm, d), tok), pl.BlockSpec((1, tm, d), tok),
                  pl.BlockSpec((1, N_MOD, d), lambda b, i: (b, 0, 0)),
                  pl.BlockSpec((d, sd), const), pl.BlockSpec((d, sd), const), pl.BlockSpec((sd, d), const),
                  pl.BlockSpec((1, d), const)],
        out_specs=pl.BlockSpec((1, tm, d), tok),
        out_shape=jax.ShapeDtypeStruct((bsz, s, d), F32),
        scratch_shapes=[pltpu.SMEM((8, N_EXPERTS), I32),
                        pltpu.VMEM((2, TOP_K * tm * ROW_TILE, LANES), U32),
                        pltpu.SemaphoreType.DMA, pltpu.SemaphoreType.DMA((2,))],
        compiler_params=_cparams(2),
        name="final",
    )(meta, meta, ys, lpos, wts, h2, x1, mod, wgs, wus, wds, g)


def _strict_lower(n):
    r = np.arange(n)
    return r[None, :] < r[:, None]


def kernel(x, c, w_ada, b_ada, g_pre_mix, g_post_mix, w_in, attn_sinks, rel_bias, w_branch_a, w_branch_b,
           w_out, g_pre_ffn, g_post_ffn, w_router, router_bias, w_gate_e, w_up_e, w_down_e,
           w_gate_s, w_up_s, w_down_s):
    bsz, s, d = x.shape
    t = bsz * s
    depth = w_ada.shape[0]
    bucket_hot = (_t5_buckets()[None] == np.arange(NUM_BUCKETS)[:, None, None]).astype(np.float32)
    bias_tab = jnp.sum(bucket_hot[:, None] * rel_bias.astype(F32)[:, :, None, None], axis=0)
    group = SWA_HEADS // SWA_KV_HEADS
    bias_tab = jnp.stack([jnp.concatenate([bias_tab[group * kv + r], bias_tab[group * kv + 2 + r]], axis=0)
                          for kv in range(SWA_KV_HEADS) for r in range(2)])
    sb_tri = jnp.asarray(_strict_lower(SB_T), BF16)
    pos_tri = jnp.asarray(_strict_lower(min(POS_TM, s)).T, BF16)
    expert_low = jnp.asarray(_strict_lower(N_EXPERTS), BF16)
    for l in range(depth):
        wa_hi, wa_lo = _split(w_ada[l])
        mod = _ada(c, wa_hi, wa_lo, b_ada[l][None, :]).reshape(bsz, N_MOD, d)
        qa, ka, va, qb, kb, vb, ga, gb = _inproj(x, mod, g_pre_mix[l][None, :], w_in[l].astype(BF16))
        ya = _swa(qa, ka, va, bias_tab, attn_sinks[l])
        yb = _sb(qb, kb, vb, sb_tri)
        wrt_hi, wrt_lo = _split(w_router[l].T)
        x1, h2b, idx, wts, counts = _mix(
            ya, yb, ga, gb, x, mod, w_branch_a[l].astype(BF16), w_branch_b[l].astype(BF16),
            w_out[l].astype(BF16), g_post_mix[l][None, :], g_pre_ffn[l][None, :],
            wrt_hi, wrt_lo, router_bias[l][:, None])
        counts = counts[:, 0].astype(I32)
        pad_counts = (counts + MOE_BLOCK - 1) // MOE_BLOCK * MOE_BLOCK
        pad_end = jnp.cumsum(pad_counts)
        pad_start = pad_end - pad_counts
        n_blocks = -(-(t * TOP_K) // MOE_BLOCK) + N_EXPERTS
        n_fill = n_blocks * MOE_BLOCK - t * TOP_K
        pad_n = pad_counts - counts
        pad_cum = jnp.cumsum(pad_n) - pad_n
        q = jnp.arange(n_fill, dtype=I32)
        own = (q[:, None] >= pad_cum[None, :]) & (q[:, None] < (pad_cum + pad_n)[None, :])
        inside = jnp.sum(jnp.where(own, (pad_start + counts - pad_cum)[None, :] + q[:, None], 0), axis=1)
        fill_pos = jnp.where(q < jnp.sum(pad_n), inside, pad_end[-1] + q - jnp.sum(pad_n))
        n_steps = t // min(POS_TM, s)
        lpos, meta, xs = _dispatch(idx, pad_start.astype(F32), pos_tri, expert_low, h2b.reshape(t, d),
                                   fill_pos.astype(I32).reshape(n_steps, 1, n_fill // n_steps))
        ys = _experts((pad_start // MOE_BLOCK).astype(I32), (pad_counts // MOE_BLOCK).astype(I32),
                      xs, w_gate_e[l], w_up_e[l], w_down_e[l])
        x = _final(meta, ys, lpos, wts, h2b, x1, mod, w_gate_s[l].astype(BF16), w_up_s[l].astype(BF16),
                   w_down_s[l].astype(BF16), g_post_ffn[l][None, :])
    return x
```

```python
import functools

import numpy as np
import jax
import jax.numpy as jnp
from jax import lax
from jax.experimental import pallas as pl
from jax.experimental.pallas import tpu as pltpu

F32 = jnp.float32
BF16 = jnp.bfloat16
I32 = jnp.int32

D_MODEL = 1024
CHUNK = 64
HEAD_DIM = 64
SWA_HEADS = 8
SWA_KV_HEADS = 2
SWA_BLOCK = 128
WINDOW_CHUNKS = 2
SB_HEADS = 8
SB_BLOCK = 128
NUM_BUCKETS = 32
MAX_DISTANCE = 128
N_EXPERTS = 256
TOP_K = 8
N_GROUPS = 8
GROUP_SIZE = N_EXPERTS // N_GROUPS
TOPK_GROUPS = 4
EXPERT_DIM = 256
ROUTED_SCALE = 2.5
MOE_BLOCK = 128
RMS_EPS = 1e-6
N_MOD = 6
NEG_INF = -1e30

QA_W = SWA_HEADS * HEAD_DIM
KVA_W = SWA_KV_HEADS * HEAD_DIM
QB_W = SB_HEADS * HEAD_DIM
IN_WIDTH = QA_W + 2 * KVA_W + 3 * QB_W + 2 * D_MODEL
LANES = 128
HEAD_PAIRS = SB_HEADS // 2
Q_SCALE = HEAD_DIM ** -0.5
LOG2_E = 1.4426950408889634

VMEM_LIMIT = 56 * 1024 * 1024


def _cparams(n_axes, vmem=VMEM_LIMIT):
    return pltpu.CompilerParams(dimension_semantics=("arbitrary",) * n_axes, vmem_limit_bytes=vmem)


def _dot(a, b):
    return jnp.dot(a, b, preferred_element_type=F32)


def _dot_t(a, b):
    return lax.dot_general(a, b, (((1,), (1,)), ((), ())), preferred_element_type=F32)


def _split(x):
    hi = x.astype(BF16)
    lo = (x - hi.astype(F32)).astype(BF16)
    return hi, lo


def _dot3(a, b_hi, b_lo):
    a_hi, a_lo = _split(a)
    return _dot(a_hi, b_hi) + (_dot(a_hi, b_lo) + _dot(a_lo, b_hi))


def _rmsnorm(x, g):
    return x * lax.rsqrt(jnp.mean(x * x, axis=-1, keepdims=True) + RMS_EPS) * g


def _silu(x):
    return x * jax.nn.sigmoid(x)


U32 = jnp.uint32
ROW_TILE = D_MODEL // (2 * LANES)
HI_HALF = 0xFFFF0000


def _store_row_tiles(ref, x, n, exact=False):
    half = x.shape[1] // 2
    if exact:
        words = pltpu.bitcast(x[:, half:], U32) | (pltpu.bitcast(x[:, :half], U32) >> 16)
    else:
        lo = pltpu.bitcast(x[:, :half].astype(BF16).astype(F32), U32) >> 16
        hi = pltpu.bitcast(x[:, half:].astype(BF16).astype(F32), U32) & jnp.uint32(HI_HALF)
        words = hi | lo
    for c in range(ROW_TILE):
        ref[pl.ds(c, n, stride=ROW_TILE), :] = words[:, c * LANES:(c + 1) * LANES]


def _load_row_tiles(ref, n):
    words = jnp.concatenate([ref[pl.ds(c, n, stride=ROW_TILE), :] for c in range(ROW_TILE)], axis=1)
    lo = pltpu.bitcast(words << 16, F32)
    hi = pltpu.bitcast(words & jnp.uint32(HI_HALF), F32)
    return jnp.concatenate([lo, hi], axis=1)


def _ada_kernel(c_ref, wh_ref, wl_ref, b_ref, o_ref):
    o_ref[...] = _dot3(_silu(c_ref[...]), wh_ref[...], wl_ref[...]) + b_ref[...]


def _ada(c, w_hi, w_lo, b):
    bsz, d = c.shape
    n = w_hi.shape[1] // d
    return pl.pallas_call(
        _ada_kernel,
        grid=(n,),
        in_specs=[pl.BlockSpec((bsz, d), lambda j: (0, 0)),
                  pl.BlockSpec((d, d), lambda j: (0, j)),
                  pl.BlockSpec((d, d), lambda j: (0, j)),
                  pl.BlockSpec((1, d), lambda j: (0, j))],
        out_specs=pl.BlockSpec((bsz, d), lambda j: (0, j)),
        out_shape=jax.ShapeDtypeStruct((bsz, n * d), F32),
        compiler_params=_cparams(1),
        name="ada",
    )(c, w_hi, w_lo, b)


INPROJ_TM = 512


def _inproj_kernel(x_ref, mod_ref, g_ref, w_ref,
                   qa_ref, ka_ref, va_ref, qb_ref, kb_ref, vb_ref, ga_ref, gb_ref):
    x = x_ref[0]
    h = _rmsnorm(x, g_ref[...]) * (1.0 + mod_ref[0, 1:2, :]) + mod_ref[0, 0:1, :]
    hb = h.astype(BF16)
    tm = x.shape[0]
    lo_half = lax.broadcasted_iota(I32, (tm, LANES), 1) < HEAD_DIM

    def proj(c0, n):
        return _dot(hb, w_ref[:, c0:c0 + n])

    def put_q(dst, base, scale):
        for c in range(0, QA_W, 256):
            dst[0, :, c:c + 256] = (proj(base + c, 256) * scale).astype(BF16)

    put_q(qa_ref, 0, Q_SCALE)
    r = proj(QA_W, 2 * KVA_W)
    for src, dst in ((r[:, :LANES], ka_ref), (r[:, LANES:], va_ref)):
        rolled = pltpu.roll(src, HEAD_DIM, axis=1)
        dst[0, :, 0 * LANES:1 * LANES] = jnp.where(lo_half, src, 0.0).astype(BF16)
        dst[0, :, 1 * LANES:2 * LANES] = jnp.where(lo_half, 0.0, rolled).astype(BF16)
        dst[0, :, 2 * LANES:3 * LANES] = jnp.where(lo_half, rolled, 0.0).astype(BF16)
        dst[0, :, 3 * LANES:4 * LANES] = jnp.where(lo_half, 0.0, src).astype(BF16)
    base_qb = QA_W + 2 * KVA_W
    put_q(qb_ref, base_qb, Q_SCALE * LOG2_E)
    for dst, base in ((kb_ref, base_qb + QB_W), (vb_ref, base_qb + 2 * QB_W)):
        for c in range(0, QB_W, 256):
            r = proj(base + c, 256)
            for t in range(2):
                pair = r[:, t * LANES:(t + 1) * LANES]
                o = 2 * (c + t * LANES)
                dst[0, :, o:o + LANES] = jnp.where(lo_half, pair, 0.0).astype(BF16)
                dst[0, :, o + LANES:o + 2 * LANES] = jnp.where(lo_half, 0.0, pair).astype(BF16)
    base_g = base_qb + 3 * QB_W
    for dst, base in ((ga_ref, base_g), (gb_ref, base_g + D_MODEL)):
        for c in range(0, D_MODEL, 256):
            dst[0, :, c:c + 256] = jax.nn.sigmoid(proj(base + c, 256)).astype(BF16)


def _inproj(x, mod, g, w_in):
    bsz, s, d = x.shape
    tm = min(INPROJ_TM, s)
    widths = (QA_W, 4 * LANES, 4 * LANES, QB_W, 2 * QB_W, 2 * QB_W, d, d)
    return pl.pallas_call(
        _inproj_kernel,
        grid=(bsz, s // tm),
        in_specs=[pl.BlockSpec((1, tm, d), lambda b, i: (b, i, 0)),
                  pl.BlockSpec((1, N_MOD, d), lambda b, i: (b, 0, 0)),
                  pl.BlockSpec((1, d), lambda b, i: (0, 0)),
                  pl.BlockSpec((d, IN_WIDTH), lambda b, i: (0, 0))],
        out_specs=[pl.BlockSpec((1, tm, w), lambda b, i: (b, i, 0)) for w in widths],
        out_shape=[jax.ShapeDtypeStruct((bsz, s, w), BF16) for w in widths],
        compiler_params=_cparams(2),
        name="inproj",
    )(x, mod, g, w_in)


def _t5_buckets():
    i = np.arange(SWA_BLOCK)[:, None]
    j = np.arange(2 * SWA_BLOCK)[None, :]
    rel = (j - SWA_BLOCK) - i
    nb = NUM_BUCKETS // 2
    bucket = (rel > 0).astype(np.int32) * nb
    n = np.abs(rel)
    max_exact = nb // 2
    large = max_exact + (np.log(np.maximum(n, 1) / max_exact)
                         / np.log(MAX_DISTANCE / max_exact) * (nb - max_exact)).astype(np.int32)
    large = np.minimum(large, nb - 1)
    return (bucket + np.where(n < max_exact, n, large)).astype(np.int32)


def _swa_kernel(sink_ref, q_ref, kp_ref, kc_ref, vp_ref, vc_ref, bias_ref, o_ref):
    n = pl.program_id(1)
    shape = (2 * SWA_BLOCK, 2 * SWA_BLOCK)
    row = lax.broadcasted_iota(I32, shape, 0)
    row_hi = (row % SWA_BLOCK) // CHUNK
    col = lax.broadcasted_iota(I32, shape, 1)
    col_chunk = col // CHUNK
    valid = (col_chunk >= row_hi) & (col_chunk <= row_hi + WINDOW_CHUNKS)
    valid = valid & ((n > 0) | (col >= SWA_BLOCK))
    first_head = lax.broadcasted_iota(I32, (2 * SWA_BLOCK, 1), 0) < SWA_BLOCK
    group = SWA_HEADS // SWA_KV_HEADS
    for kv in range(SWA_KV_HEADS):
        q = jnp.concatenate([q_ref[0, :, (2 * kv) * LANES:(2 * kv + 1) * LANES],
                             q_ref[0, :, (2 * kv + 1) * LANES:(2 * kv + 2) * LANES]], axis=0)
        acc = jnp.zeros((2 * SWA_BLOCK, LANES), F32)
        for r in range(2):
            slot = 2 * kv + r
            sl = slice(slot * LANES, (slot + 1) * LANES)
            kcat = jnp.concatenate([kp_ref[0, :, sl], kc_ref[0, :, sl]], axis=0)
            logits = jnp.where(valid, _dot_t(q, kcat) + bias_ref[slot], NEG_INF)
            sink = jnp.where(first_head, sink_ref[group * kv + r], sink_ref[group * kv + 2 + r])
            m = jnp.maximum(jnp.max(logits, axis=-1, keepdims=True), sink)
            e = jnp.exp(logits - m)
            den = jnp.sum(e, axis=-1, keepdims=True) + jnp.exp(sink - m)
            probs = (e / den).astype(BF16)
            vcat = jnp.concatenate([vp_ref[0, :, sl], vc_ref[0, :, sl]], axis=0)
            acc = acc + _dot(probs, vcat)
        o_ref[0, :, (2 * kv) * LANES:(2 * kv + 1) * LANES] = acc[:SWA_BLOCK].astype(BF16)
        o_ref[0, :, (2 * kv + 1) * LANES:(2 * kv + 2) * LANES] = acc[SWA_BLOCK:].astype(BF16)


def _swa(qa, ka, va, bias, sinks):
    bsz, s, _ = qa.shape
    nb = s // SWA_BLOCK
    cur = lambda b, n: (b, n, 0)
    prev = lambda b, n: (b, jnp.maximum(n - 1, 0), 0)
    blk = (1, SWA_BLOCK, 4 * LANES)
    return pl.pallas_call(
        _swa_kernel,
        grid=(bsz, nb),
        in_specs=[pl.BlockSpec(memory_space=pltpu.SMEM),
                  pl.BlockSpec(blk, cur),
                  pl.BlockSpec(blk, prev), pl.BlockSpec(blk, cur),
                  pl.BlockSpec(blk, prev), pl.BlockSpec(blk, cur),
                  pl.BlockSpec((2 * SWA_KV_HEADS, 2 * SWA_BLOCK, 2 * SWA_BLOCK), lambda b, n: (0, 0, 0))],
        out_specs=pl.BlockSpec(blk, cur),
        out_shape=jax.ShapeDtypeStruct((bsz, s, QA_W), BF16),
        compiler_params=_cparams(2),
        name="swa",
    )(sinks, qa, ka, ka, va, va, bias)


SB_T = 2 * SB_BLOCK


def _sb_kernel(q_ref, k_ref, v_ref, tri_ref, o_ref, acc_ref, c_ref):
    i = pl.program_id(1)
    tri = tri_ref[...]
    shape = (SB_T, SB_T)
    causal = lax.broadcasted_iota(I32, shape, 1) < lax.broadcasted_iota(I32, shape, 0)
    acc_ref[...] = jnp.zeros_like(acc_ref)
    c_ref[...] = jnp.zeros_like(c_ref)

    def stack(ref, r0, p):
        t = ref[0, pl.ds(r0, SB_T), 2 * p * LANES:(2 * p + 2) * LANES]
        return jnp.concatenate([t[:, :LANES], t[:, LANES:]], axis=0)

    def sweep(j, diag):
        r0 = pl.multiple_of(j * SB_T, SB_T)
        lws, nlks = [], []
        for p in range(HEAD_PAIRS):
            q = q_ref[0, :, p * LANES:(p + 1) * LANES]
            z2 = _dot_t(q, stack(k_ref, r0, p))
            for r in range(2):
                h = 2 * p + r
                z = z2[:, r * SB_T:(r + 1) * SB_T]
                nlk = jnp.maximum(z, 0.0) + jnp.log2(1.0 + jnp.exp2(-jnp.abs(z)))
                if diag:
                    nlk = jnp.where(causal, nlk, 0.0)
                c = c_ref[h]
                lw = z - nlk
                lws.append(jnp.concatenate([lw[:, :LANES] - c, lw[:, LANES:] - c], axis=1))
                nlks.append(nlk.astype(BF16))
                c_ref[h] = c + jnp.sum(nlk, axis=-1, keepdims=True)
        rest = _dot(jnp.concatenate(nlks, axis=0), tri)
        for p in range(HEAD_PAIRS):
            ws = []
            for r in range(2):
                h = 2 * p + r
                w = jnp.exp2(lws[h] - rest[h * SB_T:(h + 1) * SB_T])
                if diag:
                    w = jnp.where(causal, w, 0.0)
                ws.append(w.astype(BF16))
            acc_ref[p] += _dot(jnp.concatenate(ws, axis=1), stack(v_ref, r0, p))

    sweep(i, True)

    def body(t, carry):
        sweep(i - 1 - t, False)
        return carry

    lax.fori_loop(0, i, body, 0)
    for p in range(HEAD_PAIRS):
        o_ref[0, :, p * LANES:(p + 1) * LANES] = acc_ref[p].astype(BF16)


def _sb(qb, kb, vb, tri):
    bsz, s, _ = qb.shape
    return pl.pallas_call(
        _sb_kernel,
        grid=(bsz, s // SB_T),
        in_specs=[pl.BlockSpec((1, SB_T, QB_W), lambda b, i: (b, i, 0)),
                  pl.BlockSpec((1, s, 2 * QB_W), lambda b, i: (b, 0, 0)),
                  pl.BlockSpec((1, s, 2 * QB_W), lambda b, i: (b, 0, 0)),
                  pl.BlockSpec((SB_T, SB_T), lambda b, i: (0, 0))],
        out_specs=pl.BlockSpec((1, SB_T, QB_W), lambda b, i: (b, i, 0)),
        out_shape=jax.ShapeDtypeStruct((bsz, s, QB_W), BF16),
        scratch_shapes=[pltpu.VMEM((HEAD_PAIRS, SB_T, LANES), F32),
                        pltpu.VMEM((SB_HEADS, SB_T, LANES), F32)],
        compiler_params=_cparams(2),
        name="sb",
    )(qb, kb, vb, tri)


MIX_TM = 512


def _route_t(scores, bias):
    neg = -jnp.inf
    n_e, tm = scores.shape
    choice = scores + bias
    group_rows = [choice[g * GROUP_SIZE:(g + 1) * GROUP_SIZE, :] for g in range(N_GROUPS)]
    gscore = []
    for rows in group_rows:
        m1 = jnp.max(rows, axis=0, keepdims=True)
        top = rows == m1
        n_top = jnp.sum(top.astype(F32), axis=0, keepdims=True)
        m2 = jnp.max(jnp.where(top, neg, rows), axis=0, keepdims=True)
        gscore.append(m1 + jnp.where(n_top >= 2.0, m1, m2))
    parts = []
    for g, rows in enumerate(group_rows):
        rank = jnp.zeros((1, tm), F32)
        for o in range(N_GROUPS):
            if o != g:
                beats = (gscore[o] >= gscore[g]) if o < g else (gscore[o] > gscore[g])
                rank = rank + beats.astype(F32)
        parts.append(jnp.where(rank < float(TOPK_GROUPS), rows, neg))
    masked = jnp.concatenate(parts, axis=0)
    row = lax.broadcasted_iota(I32, (n_e, tm), 0).astype(F32)
    idx_rows, w_rows = [], []
    sel = jnp.zeros((n_e, tm), jnp.bool_)
    for _ in range(TOP_K):
        m = jnp.max(masked, axis=0, keepdims=True)
        first = jnp.min(jnp.where(masked == m, row, float(n_e)), axis=0, keepdims=True)
        hit = row == first
        idx_rows.append(first)
        w_rows.append(jnp.sum(jnp.where(hit, scores, 0.0), axis=0, keepdims=True))
        masked = jnp.where(hit, neg, masked)
        sel = sel | hit
    return idx_rows, w_rows, sel


def _mix_kernel(ya_ref, yb_ref, ga_ref, gb_ref, x_ref, mod_ref, wa_ref, wb_ref, wo_ref,
                gpm_ref, gpf_ref, wrh_ref, wrl_ref, rb_ref,
                x1_ref, h2b_ref, idx_ref, wt_ref, cnt_ref):
    first = (pl.program_id(0) == 0) & (pl.program_id(1) == 0)
    merged = (ga_ref[0].astype(F32) * _dot(ya_ref[0], wa_ref[...])
              + gb_ref[0].astype(F32) * _dot(yb_ref[0], wb_ref[...]))
    o = _dot(merged.astype(BF16), wo_ref[...])
    x1 = x_ref[0] + mod_ref[0, 2:3, :] * _rmsnorm(o, gpm_ref[...])
    h2 = _rmsnorm(x1, gpf_ref[...]) * (1.0 + mod_ref[0, 4:5, :]) + mod_ref[0, 3:4, :]
    x1_ref[0] = x1
    h2b_ref[0] = h2.astype(BF16)
    h_hi, h_lo = _split(h2)
    w_hi = wrh_ref[...]
    logits = _dot_t(w_hi, h_hi) + (_dot_t(w_hi, h_lo) + _dot_t(wrl_ref[...], h_hi))
    idx_rows, w_rows, sel = _route_t(jax.nn.sigmoid(logits), rb_ref[...])
    wsum = w_rows[0]
    for r in w_rows[1:]:
        wsum = wsum + r
    for k in range(TOP_K):
        idx_ref[k:k + 1, :] = idx_rows[k].astype(I32)
        wt_ref[k:k + 1, :] = w_rows[k] / wsum * ROUTED_SCALE

    @pl.when(first)
    def _():
        cnt_ref[...] = jnp.zeros_like(cnt_ref)

    cnt_ref[...] += jnp.sum(sel.astype(F32), axis=1, keepdims=True)


def _mix(ya, yb, ga, gb, x, mod, wa, wb, wo, gpm, gpf, wrt_hi, wrt_lo, rbias):
    bsz, s, d = x.shape
    tm = min(MIX_TM, s)
    nt = s // tm
    tok = lambda b, i: (b, i, 0)
    flat = lambda b, i: (0, b * nt + i)
    const = lambda b, i: (0, 0)
    return pl.pallas_call(
        _mix_kernel,
        grid=(bsz, s // tm),
        in_specs=[pl.BlockSpec((1, tm, QA_W), tok), pl.BlockSpec((1, tm, QB_W), tok),
                  pl.BlockSpec((1, tm, d), tok), pl.BlockSpec((1, tm, d), tok),
                  pl.BlockSpec((1, tm, d), tok),
                  pl.BlockSpec((1, N_MOD, d), lambda b, i: (b, 0, 0)),
                  pl.BlockSpec((QA_W, d), const), pl.BlockSpec((QB_W, d), const),
                  pl.BlockSpec((d, d), const),
                  pl.BlockSpec((1, d), const), pl.BlockSpec((1, d), const),
                  pl.BlockSpec((N_EXPERTS, d), const), pl.BlockSpec((N_EXPERTS, d), const),
                  pl.BlockSpec((N_EXPERTS, 1), const)],
        out_specs=[pl.BlockSpec((1, tm, d), tok),
                   pl.BlockSpec((1, tm, d), tok),
                   pl.BlockSpec((TOP_K, tm), flat), pl.BlockSpec((TOP_K, tm), flat),
                   pl.BlockSpec((N_EXPERTS, 1), const)],
        out_shape=[jax.ShapeDtypeStruct((bsz, s, d), F32),
                   jax.ShapeDtypeStruct((bsz, s, d), BF16),
                   jax.ShapeDtypeStruct((TOP_K, bsz * s), I32), jax.ShapeDtypeStruct((TOP_K, bsz * s), F32),
                   jax.ShapeDtypeStruct((N_EXPERTS, 1), F32)],
        compiler_params=_cparams(2),
        name="mix",
    )(ya, yb, ga, gb, x, mod, wa, wb, wo, gpm, gpf, wrt_hi, wrt_lo, rbias)


POS_TM = 512


DSP_UNROLL = 8


DSP_CHUNK = 256


def _dispatch_kernel(idx_ref, startrow_ref, tri_ref, low_ref, h_ref, fill_ref, lpos_ref, meta_ref, xs_ref,
                     carryrow_ref, zero_ref, stage_ref, meta_smem, fill_smem, sem_p, sem_f, sem_s):
    i = pl.program_id(0)
    tm = idx_ref.shape[1]
    n_fill = fill_ref.shape[2]
    n_rows = TOP_K * tm
    buf = i % 2
    stage = stage_ref.at[buf]

    def stage_wait(b):
        pltpu.make_async_copy(stage_ref.at[b], xs_ref.at[pl.ds(0, n_rows * ROW_TILE), :], sem_s.at[b]).wait()

    @pl.when(i == 0)
    def _():
        carryrow_ref[...] = jnp.zeros_like(carryrow_ref)
        zero_ref[...] = jnp.zeros_like(zero_ref)

    @pl.when(i >= 2)
    def _():
        stage_wait(buf)

    fill_to_smem = pltpu.make_async_copy(fill_ref.at[0], fill_smem, sem_p)
    fill_to_smem.start()
    fill_to_smem.wait()

    def issue_fill(r8, carry):
        for j in range(8):
            dst0 = pl.multiple_of(fill_smem[0, r8 * 8 + j] * ROW_TILE, ROW_TILE)
            pltpu.make_async_copy(zero_ref, xs_ref.at[pl.ds(dst0, ROW_TILE), :], sem_f).start(priority=j % 2)
        return carry

    lax.fori_loop(0, n_fill // 8, issue_fill, 0)

    row = lax.broadcasted_iota(I32, (N_EXPERTS, tm), 0)
    hits = [row == idx_ref[k:k + 1, :] for k in range(TOP_K)]
    sel = hits[0]
    for h in hits[1:]:
        sel = sel | h
    selb = sel.astype(BF16)
    rank = _dot(selb, tri_ref[...])
    below = _dot(low_ref[...], selb)
    local = rank + jnp.sum(below, axis=1, keepdims=True)
    local_rows = [jnp.sum(jnp.where(h, local, 0.0), axis=0, keepdims=True) for h in hits]
    for k in range(TOP_K):
        lpos_ref[k:k + 1, :] = local_rows[k]
    ones = jnp.ones((8, tm), BF16)
    count_row = _dot_t(ones, selb)[0:1]
    meta_ref[0] = jnp.zeros(meta_ref.shape[1:], I32)
    meta_ref[0, 0:1, :] = count_row.astype(I32)
    meta_ref[0, 1:2, :] = _dot_t(ones, below.astype(BF16))[0:1].astype(I32)
    meta_ref[0, 2:3, :] = (carryrow_ref[...] + startrow_ref[...]).astype(I32)
    carryrow_ref[...] += count_row
    meta_to_smem = pltpu.make_async_copy(meta_ref.at[0], meta_smem, sem_p)
    meta_to_smem.start()

    hb = h_ref[...]
    want = lax.broadcasted_iota(I32, (DSP_CHUNK, tm), 0).astype(F32).astype(BF16)
    one, zero = jnp.ones((), BF16), jnp.zeros((), BF16)
    for c in range(n_rows // DSP_CHUNK):
        lo = float(c * DSP_CHUNK)
        rel = [jnp.where((r >= lo) & (r < lo + DSP_CHUNK), r - lo, -1.0).astype(BF16) for r in local_rows]
        hot = rel[0] == want
        for r in rel[1:]:
            hot = hot | (r == want)
        rows = _dot(jnp.where(hot, one, zero), hb)
        _store_row_tiles(stage.at[pl.ds(c * DSP_CHUNK * ROW_TILE, DSP_CHUNK * ROW_TILE), :],
                         rows, DSP_CHUNK, exact=True)
    meta_to_smem.wait()

    def run(e2, carry):
        for prio in range(2):
            e = 2 * e2 + prio
            n = meta_smem[0, e] * ROW_TILE

            @pl.when(n > 0)
            def _(e=e, n=n, prio=prio):
                src0 = pl.multiple_of(meta_smem[1, e] * ROW_TILE, ROW_TILE)
                dst0 = pl.multiple_of(meta_smem[2, e] * ROW_TILE, ROW_TILE)
                pltpu.make_async_copy(stage.at[pl.ds(src0, n), :], xs_ref.at[pl.ds(dst0, n), :],
                                      sem_s.at[buf]).start(priority=prio)
        return carry

    lax.fori_loop(0, N_EXPERTS // 2, run, 0)
    for n in [n_rows] * (n_fill // n_rows) + [n_fill % n_rows]:
        if n:
            pltpu.make_async_copy(stage_ref.at[0, pl.ds(0, n * ROW_TILE), :],
                                  xs_ref.at[pl.ds(0, n * ROW_TILE), :], sem_f).wait()

    @pl.when(i == pl.num_programs(0) - 1)
    def _():
        stage_wait(buf)

        @pl.when(i >= 1)
        def _():
            stage_wait(1 - buf)


def _dispatch(idx, start, tri, low, h2b, fill_pos):
    t, d = h2b.shape
    tm = tri.shape[0]
    n_steps, _, n_fill = fill_pos.shape
    n_rows = t * TOP_K + n_steps * n_fill
    return pl.pallas_call(
        _dispatch_kernel,
        grid=(n_steps,),
        in_specs=[pl.BlockSpec((TOP_K, tm), lambda i: (0, i)),
                  pl.BlockSpec((1, N_EXPERTS), lambda i: (0, 0)),
                  pl.BlockSpec((tm, tm), lambda i: (0, 0)),
                  pl.BlockSpec((N_EXPERTS, N_EXPERTS), lambda i: (0, 0)),
                  pl.BlockSpec((tm, d), lambda i: (i, 0)),
                  pl.BlockSpec((1, 1, n_fill), lambda i: (i, 0, 0))],
        out_specs=[pl.BlockSpec((TOP_K, tm), lambda i: (0, i)),
                   pl.BlockSpec((1, 8, N_EXPERTS), lambda i: (i, 0, 0)),
                   pl.BlockSpec(memory_space=pl.ANY)],
        out_shape=[jax.ShapeDtypeStruct((TOP_K, t), F32),
                   jax.ShapeDtypeStruct((n_steps, 8, N_EXPERTS), I32),
                   jax.ShapeDtypeStruct((n_rows * ROW_TILE, LANES), U32)],
        scratch_shapes=[pltpu.VMEM((1, N_EXPERTS), F32),
                        pltpu.VMEM((ROW_TILE, LANES), U32),
                        pltpu.VMEM((2, TOP_K * tm * ROW_TILE, LANES), U32),
                        pltpu.SMEM((8, N_EXPERTS), I32), pltpu.SMEM((1, n_fill), I32),
                        pltpu.SemaphoreType.DMA, pltpu.SemaphoreType.DMA, pltpu.SemaphoreType.DMA((2,))],
        compiler_params=_cparams(1),
        name="dispatch",
    )(idx, start[None, :], tri, low, h2b, fill_pos)


EXP_RING = 4


def _experts_kernel(b0_ref, nb_ref, xs_ref, wg_ref, wu_ref, wd_ref, y_ref,
                    xbuf, ybuf, hid_ref, wg_b, wu_b, wd_b, sem_x, sem_y):
    e = pl.program_id(0)
    n_exp = pl.num_programs(0)
    n_used = b0_ref[n_exp - 1] + nb_ref[n_exp - 1]
    last = n_used - 1
    blk_rows = MOE_BLOCK * ROW_TILE

    def rows_of(ref, g):
        return ref.at[pl.ds(pl.multiple_of(g * blk_rows, blk_rows), blk_rows), :]

    def in_copy(g):
        return pltpu.make_async_copy(rows_of(xs_ref, jnp.minimum(g, last)), xbuf.at[g % EXP_RING],
                                     sem_x.at[g % EXP_RING])

    def out_copy(g, s):
        return pltpu.make_async_copy(ybuf.at[s], rows_of(y_ref, g), sem_y.at[s])

    def hidden(g):
        xb = _load_row_tiles(xbuf.at[g % EXP_RING], MOE_BLOCK).astype(BF16)
        return (_silu(_dot(xb, wg_b[...])) * _dot(xb, wu_b[...])).astype(BF16)

    @pl.when(e == 0)
    def _():
        for g in range(EXP_RING - 1):
            in_copy(g).start()

    wg_b[...] = wg_ref[0].astype(BF16)
    wu_b[...] = wu_ref[0].astype(BF16)
    wd_b[...] = wd_ref[0].astype(BF16)
    b0 = b0_ref[e]
    nb = nb_ref[e]

    @pl.when(nb > 0)
    def _():
        in_copy(b0).wait()
        hid_ref[...] = hidden(b0)

    def block(j, carry):
        g = b0 + j
        sy = g % 2

        @pl.when(g >= 2)
        def _():
            out_copy(g - 2, sy).wait()

        @pl.when(j + 1 < nb)
        def _():
            in_copy(g + 1).wait()

        in_copy(g + EXP_RING - 1).start()
        hid_next = hidden(jnp.minimum(g + 1, b0 + nb - 1))
        _store_row_tiles(ybuf.at[sy], _dot(hid_ref[...], wd_b[...]), MOE_BLOCK)
        out_copy(g, sy).start()
        hid_ref[...] = hid_next
        return carry

    lax.fori_loop(0, nb, block, 0)

    @pl.when(e == n_exp - 1)
    def _():
        for i in range(EXP_RING - 1):
            in_copy(n_used + i).wait()
        out_copy(last, last % 2).wait()

        @pl.when(n_used >= 2)
        def _():
            out_copy(last - 1, n_used % 2).wait()

        n_blocks = y_ref.shape[0] // blk_rows
        ybuf[0] = jnp.zeros((blk_rows, LANES), U32)

        def fill(g, carry):
            out_copy(g, 0).start()
            return carry

        def fill_wait(g, carry):
            out_copy(g, 0).wait()
            return carry

        lax.fori_loop(n_used, n_blocks, fill, 0)
        lax.fori_loop(n_used, n_blocks, fill_wait, 0)


def _experts(blk_start, blk_count, xs, wg, wu, wd):
    n_exp, d, e_dim = wg.shape
    blk_rows = MOE_BLOCK * ROW_TILE
    grid_spec = pltpu.PrefetchScalarGridSpec(
        num_scalar_prefetch=2,
        grid=(n_exp,),
        in_specs=[pl.BlockSpec(memory_space=pl.ANY),
                  pl.BlockSpec((1, d, e_dim), lambda e, b0, nb: (e, 0, 0)),
                  pl.BlockSpec((1, d, e_dim), lambda e, b0, nb: (e, 0, 0)),
                  pl.BlockSpec((1, e_dim, d), lambda e, b0, nb: (e, 0, 0))],
        out_specs=pl.BlockSpec(memory_space=pl.ANY),
        scratch_shapes=[pltpu.VMEM((EXP_RING, blk_rows, LANES), U32),
                        pltpu.VMEM((2, blk_rows, LANES), U32),
                        pltpu.VMEM((MOE_BLOCK, e_dim), BF16),
                        pltpu.VMEM((d, e_dim), BF16), pltpu.VMEM((d, e_dim), BF16),
                        pltpu.VMEM((e_dim, d), BF16),
                        pltpu.SemaphoreType.DMA((EXP_RING,)), pltpu.SemaphoreType.DMA((2,))],
    )
    return pl.pallas_call(
        _experts_kernel,
        grid_spec=grid_spec,
        out_shape=jax.ShapeDtypeStruct(xs.shape, U32),
        compiler_params=_cparams(1),
        name="experts",
    )(blk_start, blk_count, xs, wg, wu, wd)


def _final_kernel(meta0_ref, meta1_ref, ys_ref, lpos_ref, wt_ref, h2_ref, x1_ref, mod_ref,
                  wgs_ref, wus_ref, wds_ref, g_ref, o_ref, meta_smem, stage_ref, sem_m, sem_g):
    b, i = pl.program_id(0), pl.program_id(1)
    nt = pl.num_programs(1)
    step = b * nt + i
    last = pl.num_programs(0) * nt - 1
    cur = step % 2
    nxt = 1 - cur
    tm = o_ref.shape[1]
    n_rows = TOP_K * tm

    def issue_runs(meta_ref, s):
        to_smem = pltpu.make_async_copy(meta_ref.at[0], meta_smem, sem_m)
        to_smem.start()
        to_smem.wait()

        def run(e2, carry):
            for prio in range(2):
                e = 2 * e2 + prio
                n = meta_smem[0, e] * ROW_TILE

                @pl.when(n > 0)
                def _(e=e, n=n, prio=prio):
                    dst0 = pl.multiple_of(meta_smem[1, e] * ROW_TILE, ROW_TILE)
                    src0 = pl.multiple_of(meta_smem[2, e] * ROW_TILE, ROW_TILE)
                    pltpu.make_async_copy(ys_ref.at[pl.ds(src0, n), :], stage_ref.at[s, pl.ds(dst0, n), :],
                                          sem_g.at[s]).start(priority=prio)
            return carry

        lax.fori_loop(0, N_EXPERTS // 2, run, 0)

    def wait_runs(s):
        pltpu.make_async_copy(ys_ref.at[pl.ds(0, n_rows * ROW_TILE), :], stage_ref.at[s], sem_g.at[s]).wait()

    @pl.when(step == 0)
    def _():
        issue_runs(meta0_ref, 0)

    issue_runs(meta1_ref, nxt)
    hb = h2_ref[0]
    y = _dot((_silu(_dot(hb, wgs_ref[...])) * _dot(hb, wus_ref[...])).astype(BF16), wds_ref[...])
    wait_runs(cur)
    want = lax.broadcasted_iota(I32, (DSP_CHUNK, tm), 0).astype(F32).astype(BF16)
    lpos = [lpos_ref[k:k + 1, :] for k in range(TOP_K)]
    wts = [wt_ref[k:k + 1, :].astype(BF16) for k in range(TOP_K)]
    zero = jnp.zeros((), BF16)
    for c in range(n_rows // DSP_CHUNK):
        lo = float(c * DSP_CHUNK)
        comb = jnp.zeros((DSP_CHUNK, tm), BF16)
        for r, w in zip(lpos, wts):
            rel = jnp.where((r >= lo) & (r < lo + DSP_CHUNK), r - lo, -1.0).astype(BF16)
            comb = jnp.where(rel == want, w, comb)
        rows = _load_row_tiles(stage_ref.at[cur, pl.ds(c * DSP_CHUNK * ROW_TILE, DSP_CHUNK * ROW_TILE), :],
                               DSP_CHUNK).astype(BF16)
        y = y + lax.dot_general(comb, rows, (((0,), (0,)), ((), ())), preferred_element_type=F32)
    o_ref[0] = x1_ref[0] + mod_ref[0, 5:6, :] * _rmsnorm(y, g_ref[...])

    @pl.when(step == last)
    def _():
        wait_runs(nxt)


def _final(meta, ys, lpos, wts, h2, x1, mod, wgs, wus, wds, g):
    bsz, s, d = x1.shape
    tm = min(POS_TM, s)
    nt = s // tm
    n_tiles = bsz * nt
    sd = wgs.shape[1]
    tok = lambda b, i: (b, i, 0)
    flat = lambda b, i: (0, b * nt + i)
    const = lambda b, i: (0, 0)
    return pl.pallas_call(
        _final_kernel,
        grid=(bsz, nt),
        in_specs=[pl.BlockSpec((1, 8, N_EXPERTS), lambda b, i: (0, 0, 0)),
                  pl.BlockSpec((1, 8, N_EXPERTS), lambda b, i: (jnp.minimum(b * nt + i + 1, n_tiles - 1), 0, 0)),
                  pl.BlockSpec(memory_space=pl.ANY),
                  pl.BlockSpec((TOP_K, tm), flat), pl.BlockSpec((TOP_K, tm), flat),
                  pl.BlockSpec((1, tm, d), tok), pl.BlockSpec((1, tm, d), tok),
                  pl.BlockSpec((1, N_MOD, d), lambda b, i: (b, 0, 0)),
                  pl.BlockSpec((d, sd), const), pl.BlockSpec((d, sd), const), pl.BlockSpec((sd, d), const),
                  pl.BlockSpec((1, d), const)],
        out_specs=pl.BlockSpec((1, tm, d), tok),
        out_shape=jax.ShapeDtypeStruct((bsz, s, d), F32),
        scratch_shapes=[pltpu.SMEM((8, N_EXPERTS), I32),
                        pltpu.VMEM((2, TOP_K * tm * ROW_TILE, LANES), U32),
                        pltpu.SemaphoreType.DMA, pltpu.SemaphoreType.DMA((2,))],
        compiler_params=_cparams(2),
        name="final",
    )(meta, meta, ys, lpos, wts, h2, x1, mod, wgs, wus, wds, g)


def _strict_lower(n):
    r = np.arange(n)
    return r[None, :] < r[:, None]


def kernel(x, c, w_ada, b_ada, g_pre_mix, g_post_mix, w_in, attn_sinks, rel_bias, w_branch_a, w_branch_b,
           w_out, g_pre_ffn, g_post_ffn, w_router, router_bias, w_gate_e, w_up_e, w_down_e,
           w_gate_s, w_up_s, w_down_s):
    bsz, s, d = x.shape
    t = bsz * s
    depth = w_ada.shape[0]
    bucket_hot = (_t5_buckets()[None] == np.arange(NUM_BUCKETS)[:, None, None]).astype(np.float32)
    bias_tab = jnp.sum(bucket_hot[:, None] * rel_bias.astype(F32)[:, :, None, None], axis=0)
    group = SWA_HEADS // SWA_KV_HEADS
    bias_tab = jnp.stack([jnp.concatenate([bias_tab[group * kv + r], bias_tab[group * kv + 2 + r]], axis=0)
                          for kv in range(SWA_KV_HEADS) for r in range(2)])
    sb_tri = jnp.asarray(_strict_lower(SB_T), BF16)
    pos_tri = jnp.asarray(_strict_lower(min(POS_TM, s)).T, BF16)
    expert_low = jnp.asarray(_strict_lower(N_EXPERTS), BF16)
    for l in range(depth):
        wa_hi, wa_lo = _split(w_ada[l])
        mod = _ada(c, wa_hi, wa_lo, b_ada[l][None, :]).reshape(bsz, N_MOD, d)
        qa, ka, va, qb, kb, vb, ga, gb = _inproj(x, mod, g_pre_mix[l][None, :], w_in[l].astype(BF16))
        ya = _swa(qa, ka, va, bias_tab, attn_sinks[l])
        yb = _sb(qb, kb, vb, sb_tri)
        wrt_hi, wrt_lo = _split(w_router[l].T)
        x1, h2b, idx, wts, counts = _mix(
            ya, yb, ga, gb, x, mod, w_branch_a[l].astype(BF16), w_branch_b[l].astype(BF16),
            w_out[l].astype(BF16), g_post_mix[l][None, :], g_pre_ffn[l][None, :],
            wrt_hi, wrt_lo, router_bias[l][:, None])
        counts = counts[:, 0].astype(I32)
        pad_counts = (counts + MOE_BLOCK - 1) // MOE_BLOCK * MOE_BLOCK
        pad_end = jnp.cumsum(pad_counts)
        pad_start = pad_end - pad_counts
        n_blocks = -(-(t * TOP_K) // MOE_BLOCK) + N_EXPERTS
        n_fill = n_blocks * MOE_BLOCK - t * TOP_K
        pad_n = pad_counts - counts
        pad_cum = jnp.cumsum(pad_n) - pad_n
        q = jnp.arange(n_fill, dtype=I32)
        own = (q[:, None] >= pad_cum[None, :]) & (q[:, None] < (pad_cum + pad_n)[None, :])
        inside = jnp.sum(jnp.where(own, (pad_start + counts - pad_cum)[None, :] + q[:, None], 0), axis=1)
        fill_pos = jnp.where(q < jnp.sum(pad_n), inside, pad_end[-1] + q - jnp.sum(pad_n))
        n_steps = t // min(POS_TM, s)
        lpos, meta, xs = _dispatch(idx, pad_start.astype(F32), pos_tri, expert_low, h2b.reshape(t, d),
                                   fill_pos.astype(I32).reshape(n_steps, 1, n_fill // n_steps))
        ys = _experts((pad_start // MOE_BLOCK).astype(I32), (pad_counts // MOE_BLOCK).astype(I32),
                      xs, w_gate_e[l], w_up_e[l], w_down_e[l])
        x = _final(meta, ys, lpos, wts, h2b, x1, mod, w_gate_s[l].astype(BF16), w_up_s[l].astype(BF16),
                   w_down_s[l].astype(BF16), g_post_ffn[l][None, :])
    return x
```
